```python
import math
import jax
import jax.numpy as jnp
from jax import lax
import numpy as np

D_MODEL = 4096
BATCH = 1
SEQ = 8192
DEPTH = 4
DEC_BATCH = 16
DEC_SEQ = 16
PAST_LEN = 1024

CHUNK = 64
Q_BLOCK = 128
HEAD_DIM = 128
H_SB = 8
H_FOX = 8
H_MLA = 16
D_SB = 1024
D_FOX = 1024
MLA_NOPE = 128
MLA_ROPE = 64
MLA_V = 128
Q_RANK = 1024
KV_RANK = 512
D_MLA = 2048
D_MIX = 4096
IN_COLS = 7752
N_EXPERTS = 32
TOP_K = 4
D_EXPERT = 1024
SWIGLU_LIMIT = 7.0
SWIGLU_ALPHA = 1.702
ROPE_THETA = 10000.0
EPS = 1e-6
N_STATE = 7

kernel_name = 'hybrid_stream_encoder_step'


def rms(x):
    xf = x.astype(jnp.float32)
    return (xf * lax.rsqrt(jnp.mean(xf * xf, axis=-1, keepdims=True) + EPS)).astype(x.dtype)


def rmsnorm(x, g):
    return rms(x) * g


def group_norm(o, g):
    parts = jnp.split(o, [D_SB, D_SB + D_FOX], axis=-1)
    return jnp.concatenate([rms(p) for p in parts], axis=-1) * g


def modulation(c, w, b, n):
    m = jax.nn.silu(c) @ w + b
    return jnp.split(m[:, None, :], n, axis=-1)


def apply_rope(x, pos):
    half = MLA_ROPE // 2
    inv_freq = ROPE_THETA ** (-jnp.arange(half, dtype=jnp.float32) / half)
    ang = pos.astype(jnp.float32)[:, None] * inv_freq[None, :]
    shape = (pos.shape[0],) + (1,) * (x.ndim - 3) + (half,)
    cos = jnp.cos(ang).reshape(shape)
    sin = jnp.sin(ang).reshape(shape)
    xf = x.astype(jnp.float32)
    x1, x2 = xf[..., :half], xf[..., half:]
    return jnp.concatenate([x1 * cos - x2 * sin, x2 * cos + x1 * sin], axis=-1).astype(x.dtype)


def stick_breaking_core(q, k, v, qpos, kpos):
    z = jnp.einsum('bqhd,bkhd->bhqk', q.astype(jnp.float32), k.astype(jnp.float32)) / math.sqrt(HEAD_DIM)
    before = kpos[None, :] < qpos[:, None]
    log_1m = jnp.where(before, jax.nn.log_sigmoid(-z), 0.0)
    suffix = lax.cumsum(log_1m, axis=3, reverse=True) - log_1m
    w = jnp.where(before, jnp.exp(jax.nn.log_sigmoid(z) + suffix), 0.0)
    return jnp.einsum('bhqk,bkhd->bqhd', w.astype(v.dtype), v)


def forgetting_core(q, cq, k, v, ck, qpos, kpos):
    s = jnp.einsum('bqhd,bkhd->bhqk', q.astype(jnp.float32), k.astype(jnp.float32)) / math.sqrt(HEAD_DIM)
    s = s + jnp.transpose(cq, (0, 2, 1))[:, :, :, None] - jnp.transpose(ck, (0, 2, 1))[:, :, None, :]
    s = jnp.where(kpos[None, :] <= qpos[:, None], s, -jnp.inf)
    p = jax.nn.softmax(s, axis=-1)
    return jnp.einsum('bhqk,bkhd->bqhd', p.astype(v.dtype), v)


def mla_core(q_lat, q_rope, ckv, kr, qpos, kpos):
    scale = 1.0 / math.sqrt(MLA_NOPE + MLA_ROPE)
    s = (jnp.einsum('bqhc,bkc->bhqk', q_lat.astype(jnp.float32), ckv.astype(jnp.float32))
         + jnp.einsum('bqhr,bkr->bhqk', q_rope.astype(jnp.float32), kr.astype(jnp.float32))) * scale
    s = jnp.where((kpos // CHUNK)[None, :] <= (qpos // CHUNK)[:, None], s, -jnp.inf)
    p = jax.nn.softmax(s, axis=-1)
    return jnp.einsum('bhqk,bkc->bqhc', p.astype(ckv.dtype), ckv)


def attend_direct(core, q_side, k_side, qpos, kpos):
    return core(*q_side, *k_side, qpos, kpos)


def attend_blocks(core, q_side, k_side, qpos, kpos):
    T = qpos.shape[0]
    nb = T // Q_BLOCK

    def blockify(a):
        a = a.reshape((a.shape[0], nb, Q_BLOCK) + a.shape[2:])
        return jnp.moveaxis(a, 1, 0)

    qb = tuple(blockify(a) for a in q_side)
    pb = qpos.reshape(nb, Q_BLOCK)
    out = lax.map(lambda args: core(*args[0], *k_side, args[1], kpos), (qb, pb))
    out = jnp.moveaxis(out, 0, 1)
    return out.reshape((out.shape[0], T) + out.shape[3:])


def token_mixers(h, pos, past, blocked, w_in, b_forget, g_q, w_uq, g_kv, w_uk, w_uv, g_out, w_o):
    B, T, _ = h.shape
    sizes = [D_SB, D_SB, D_SB, D_FOX, D_FOX, D_FOX, H_FOX, Q_RANK, KV_RANK, MLA_ROPE]
    parts = jnp.split(h @ w_in, np.cumsum(sizes)[:-1].tolist(), axis=-1)
    q_sb, k_sb, v_sb, q_fx, k_fx, v_fx, f_logit, c_q, c_kv, k_r = parts
    q_sb = q_sb.reshape(B, T, H_SB, HEAD_DIM)
    k_sb = k_sb.reshape(B, T, H_SB, HEAD_DIM)
    v_sb = v_sb.reshape(B, T, H_SB, HEAD_DIM)
    q_fx = q_fx.reshape(B, T, H_FOX, HEAD_DIM)
    k_fx = k_fx.reshape(B, T, H_FOX, HEAD_DIM)
    v_fx = v_fx.reshape(B, T, H_FOX, HEAD_DIM)
    logf = jax.nn.log_sigmoid((f_logit + b_forget).astype(jnp.float32))
    q_m = (rmsnorm(c_q, g_q) @ w_uq).reshape(B, T, H_MLA, MLA_NOPE + MLA_ROPE)
    q_rope = apply_rope(q_m[..., MLA_NOPE:], pos)
    q_lat = jnp.einsum('bthn,chn->bthc', q_m[..., :MLA_NOPE], w_uk)
    ckv = rmsnorm(c_kv, g_kv)
    kr = apply_rope(k_r, pos)
    new_rows = (k_sb, v_sb, k_fx, v_fx, logf, ckv, kr)
    if past is None:
        keys = new_rows
        kpos = pos
    else:
        keys = tuple(jnp.concatenate([p.astype(n.dtype), n], axis=1) for p, n in zip(past, new_rows))
        kpos = jnp.arange(past[0].shape[1] + T, dtype=jnp.int32)
    kk_sb, vv_sb, kk_fx, vv_fx, lf_all, ckv_all, kr_all = keys
    cum_all = jnp.cumsum(lf_all.astype(jnp.float32), axis=1)
    cum_q = cum_all[:, cum_all.shape[1] - T:]
    attend = attend_blocks if blocked else attend_direct
    o_sb = attend(stick_breaking_core, (q_sb,), (kk_sb, vv_sb), pos, kpos)
    o_fx = attend(forgetting_core, (q_fx, cum_q), (kk_fx, vv_fx, cum_all), pos, kpos)
    o_lat = attend(mla_core, (q_lat, q_rope), (ckv_all, kr_all), pos, kpos)
    o_mla = jnp.einsum('bthc,chv->bthv', o_lat, w_uv)
    o = jnp.concatenate([o_sb.reshape(B, T, D_SB), o_fx.reshape(B, T, D_FOX),
                         o_mla.reshape(B, T, D_MLA)], axis=-1)
    return group_norm(o, g_out) @ w_o, new_rows


def moe(h, w_router, b_router, w_gate, b_gate, w_up, b_up, w_down, b_down):
    B, T, D = h.shape
    hf = h.reshape(B * T, D)
    logits = (hf @ w_router + b_router).astype(jnp.float32)
    top_val, top_idx = lax.top_k(logits, TOP_K)
    gates = jax.nn.softmax(top_val, axis=-1)
    comb = jnp.sum(jax.nn.one_hot(top_idx, N_EXPERTS, dtype=jnp.float32) * gates[..., None], axis=1)
    comb = comb.astype(h.dtype)
    out = jnp.zeros_like(hf)
    for e in range(N_EXPERTS):
        g = jnp.minimum(hf @ w_gate[e] + b_gate[e], SWIGLU_LIMIT)
        u = jnp.clip(hf @ w_up[e] + b_up[e], -SWIGLU_LIMIT, SWIGLU_LIMIT)
        y = ((u + 1.0) * g * jax.nn.sigmoid(SWIGLU_ALPHA * g)) @ w_down[e] + b_down[e]
        out = out + comb[:, e:e + 1] * y
    return out.reshape(B, T, D)


def run_trunk(x, c, pos, caches, blocked, stacked, final):
    (w_in, b_forget, g_q, w_uq, g_kv, w_uk, w_uv, g_out, w_o, g_attn, g_mlp, w_ada, b_ada,
     w_router, b_router, w_gate, b_gate, w_up, b_up, w_down, b_down) = stacked
    g_final, w_final_ada, b_final_ada = final
    rows = [[] for _ in range(N_STATE)]
    for l in range(DEPTH):
        sh_a, sc_a, gt_a, sh_m, sc_m, gt_m = modulation(c, w_ada[l], b_ada[l], 6)
        past = None if caches is None else [cc[l] for cc in caches]
        h = rmsnorm(x, g_attn[l]) * (1.0 + sc_a) + sh_a
        mix, new_rows = token_mixers(h, pos, past, blocked, w_in[l], b_forget[l], g_q[l], w_uq[l],
                                     g_kv[l], w_uk[l], w_uv[l], g_out[l], w_o[l])
        x = x + gt_a * mix
        h = rmsnorm(x, g_mlp[l]) * (1.0 + sc_m) + sh_m
        x = x + gt_m * moe(h, w_router[l], b_router[l], w_gate[l], b_gate[l], w_up[l], b_up[l],
                           w_down[l], b_down[l])
        for r, nr in zip(rows, new_rows):
            r.append(nr)
    sh_f, sc_f = modulation(c, w_final_ada, b_final_ada, 2)
    y = rmsnorm(x, g_final) * (1.0 + sc_f) + sh_f
    return y, [jnp.stack(r) for r in rows]


def setup_inputs(seed: int = 0) -> dict:
    key = jax.random.key(seed)
    ks = jax.random.split(key, 40)
    f32 = jnp.float32

    def nrm(k, shape, scale=1.0):
        return scale * jax.random.normal(k, shape, f32)

    def gain(k, shape):
        return 1.0 + 0.02 * jax.random.normal(k, shape, f32)

    P = PAST_LEN
    L = DEPTH
    D = D_MODEL
    return {
        'x_prompt': nrm(ks[0], (BATCH, SEQ, D)),
        'x_sample': nrm(ks[1], (DEC_BATCH, DEC_SEQ, D)),
        'c_prompt': nrm(ks[2], (BATCH, D)),
        'c_sample': nrm(ks[3], (DEC_BATCH, D)),
        'cache_sb_k': nrm(ks[4], (L, DEC_BATCH, P, H_SB, HEAD_DIM)),
        'cache_sb_v': nrm(ks[5], (L, DEC_BATCH, P, H_SB, HEAD_DIM)),
        'cache_fox_k': nrm(ks[6], (L, DEC_BATCH, P, H_FOX, HEAD_DIM)),
        'cache_fox_v': nrm(ks[7], (L, DEC_BATCH, P, H_FOX, HEAD_DIM)),
        'cache_fox_logf': jax.nn.log_sigmoid(2.0 + nrm(ks[8], (L, DEC_BATCH, P, H_FOX))),
        'cache_mla_ckv': nrm(ks[9], (L, DEC_BATCH, P, KV_RANK)),
        'cache_mla_krope': nrm(ks[10], (L, DEC_BATCH, P, MLA_ROPE)),
        'w_in': nrm(ks[11], (L, D, IN_COLS), D ** -0.5),
        'b_forget': 2.0 + nrm(ks[12], (L, H_FOX), 0.1),
        'g_q': gain(ks[13], (L, Q_RANK)),
        'w_uq': nrm(ks[14], (L, Q_RANK, H_MLA * (MLA_NOPE + MLA_ROPE)), Q_RANK ** -0.5),
        'g_kv': gain(ks[15], (L, KV_RANK)),
        'w_uk': nrm(ks[16], (L, KV_RANK, H_MLA, MLA_NOPE), KV_RANK ** -0.5),
        'w_uv': nrm(ks[17], (L, KV_RANK, H_MLA, MLA_V), KV_RANK ** -0.5),
        'g_out': gain(ks[18], (L, D_MIX)),
        'w_o': nrm(ks[19], (L, D_MIX, D), D_MIX ** -0.5),
        'g_attn': gain(ks[20], (L, D)),
        'g_mlp': gain(ks[21], (L, D)),
        'w_ada': nrm(ks[22], (L, D, 6 * D), 0.5 * D ** -0.5),
        'b_ada': nrm(ks[23], (L, 6 * D), 0.02),
        'w_router': nrm(ks[24], (L, D, N_EXPERTS), D ** -0.5),
        'b_router': nrm(ks[25], (L, N_EXPERTS), 0.01),
        'w_gate': nrm(ks[26], (L, N_EXPERTS, D, D_EXPERT), D ** -0.5),
        'b_gate': nrm(ks[27], (L, N_EXPERTS, D_EXPERT), 0.02),
        'w_up': nrm(ks[28], (L, N_EXPERTS, D, D_EXPERT), D ** -0.5),
        'b_up': nrm(ks[29], (L, N_EXPERTS, D_EXPERT), 0.02),
        'w_down': nrm(ks[30], (L, N_EXPERTS, D_EXPERT, D), D_EXPERT ** -0.5),
        'b_down': nrm(ks[31], (L, N_EXPERTS, D), 0.02),
        'g_final': gain(ks[32], (D,)),
        'w_final_ada': nrm(ks[33], (D, 2 * D), 0.5 * D ** -0.5),
        'b_final_ada': nrm(ks[34], (2 * D,), 0.02),
    }


def reference(x_prompt, x_sample, c_prompt, c_sample, cache_sb_k, cache_sb_v, cache_fox_k, cache_fox_v,
              cache_fox_logf, cache_mla_ckv, cache_mla_krope, w_in, b_forget, g_q, w_uq, g_kv, w_uk, w_uv,
              g_out, w_o, g_attn, g_mlp, w_ada, b_ada, w_router, b_router, w_gate, b_gate, w_up, b_up,
              w_down, b_down, g_final, w_final_ada, b_final_ada):
    stacked = (w_in, b_forget, g_q, w_uq, g_kv, w_uk, w_uv, g_out, w_o, g_attn, g_mlp, w_ada, b_ada,
               w_router, b_router, w_gate, b_gate, w_up, b_up, w_down, b_down)
    final = (g_final, w_final_ada, b_final_ada)
    caches = (cache_sb_k, cache_sb_v, cache_fox_k, cache_fox_v, cache_fox_logf, cache_mla_ckv, cache_mla_krope)
    pos_p = jnp.arange(x_prompt.shape[1], dtype=jnp.int32)
    pos_s = cache_sb_k.shape[2] + jnp.arange(x_sample.shape[1], dtype=jnp.int32)
    y_prompt, st_p = run_trunk(x_prompt, c_prompt, pos_p, None, True, stacked, final)
    y_sample, st_s = run_trunk(x_sample, c_sample, pos_s, caches, False, stacked, final)
    sb_k_p, sb_v_p, fox_k_p, fox_v_p, fox_logf_p, mla_ckv_p, mla_krope_p = st_p
    sb_k_s, sb_v_s, fox_k_s, fox_v_s, fox_logf_s, mla_ckv_s, mla_krope_s = st_s
    return (y_prompt, y_sample, sb_k_p, sb_v_p, fox_k_p, fox_v_p, fox_logf_p, mla_ckv_p, mla_krope_p,
            sb_k_s, sb_v_s, fox_k_s, fox_v_s, fox_logf_s, mla_ckv_s, mla_krope_s)
```

```python
import functools
import math

import jax
import jax.numpy as jnp
from jax import lax
from jax.experimental import pallas as pl
from jax.experimental.pallas import tpu as pltpu

F32 = jnp.float32
BF16 = jnp.bfloat16
I32 = jnp.int32
U32 = jnp.uint32

HEAD_DIM = 128
LANES = 128
CHUNK = 64
TOP_K = 4
SWIGLU_LIMIT = 7.0
SWIGLU_ALPHA = 1.702
ROPE_THETA = 10000.0
EPS = 1e-6
NEG_BIG = -1e30
SB_DEAD = -104.0
MOE_TILE = 256
MODROWS = 32
MIB = 1024 * 1024


def _cparams(sem, vmem_mib):
    return pltpu.CompilerParams(dimension_semantics=sem, vmem_limit_bytes=vmem_mib * MIB)


def _pick_tile(n, target, align):
    best = None
    for t in range(align, min(n, target) + 1, align):
        if n % t == 0:
            best = t
    assert best is not None, (n, target, align)
    return best


def _rms(x):
    return x * lax.rsqrt(jnp.mean(x * x, axis=-1, keepdims=True) + EPS)


def _sigmoid(x):
    return 1.0 / (1.0 + jnp.exp(-x))


def _softplus(z):
    return jnp.maximum(z, 0.0) + jnp.log1p(jnp.exp(-jnp.abs(z)))


def _dot(a, b):
    return jnp.dot(a, b, preferred_element_type=F32)


def _dot_nt(a, b):
    return lax.dot_general(a, b, (((1,), (1,)), ((), ())), preferred_element_type=F32)


def _split2(x):
    hi = x.astype(BF16)
    lo = (x - hi.astype(F32)).astype(BF16)
    return hi, lo


def _ada_kernel(c_ref, w_ref, b_ref, o_ref):
    c = c_ref[...]
    s = (c * _sigmoid(c)).astype(BF16)
    o_ref[...] = _dot(s, w_ref[...].astype(BF16)) + b_ref[...]


def _ada(c_all, w, b):
    nl, d, n = w.shape
    tn = _pick_tile(n, 512, LANES)
    return pl.pallas_call(
        _ada_kernel,
        grid=(nl, n // tn),
        in_specs=[
            pl.BlockSpec((MODROWS, d), lambda l, j: (0, 0)),
            pl.BlockSpec((None, d, tn), lambda l, j: (l, 0, j)),
            pl.BlockSpec((None, 1, tn), lambda l, j: (l, 0, j)),
        ],
        out_specs=pl.BlockSpec((None, MODROWS, tn), lambda l, j: (l, 0, j)),
        out_shape=jax.ShapeDtypeStruct((nl, MODROWS, n), F32),
        compiler_params=_cparams(("parallel", "parallel"), 40),
    )(c_all, w, b.reshape(nl, 1, n))


def _expand_rows(m, ts):
    nb, d = m.shape
    return jnp.broadcast_to(m[:, None, :], (nb, ts, d)).reshape(nb * ts, d)


def _modulated(i, npt, tm, ts, prow, refs, fn):
    nb = tm // ts

    @pl.when(i < npt)
    def _():
        fn(*[r[prow:prow + 1, :] for r in refs])

    @pl.when(i >= npt)
    def _():
        b0 = pl.multiple_of((i - npt) * nb, nb)
        fn(*[_expand_rows(r[pl.ds(b0, nb), :], ts) for r in refs])


def _normmod_kernel(x_ref, g_ref, sc_ref, sh_ref, *o_refs, npt, tm, ts, prow, split_out):
    i = pl.program_id(0)
    xn = _rms(x_ref[...]) * g_ref[...]

    def emit(sc, sh):
        y = xn * (1.0 + sc) + sh
        if split_out:
            op_ref, os_ref = o_refs

            @pl.when(i < npt)
            def _():
                op_ref[...] = y.astype(op_ref.dtype)

            @pl.when(i >= npt)
            def _():
                os_ref[...] = y.astype(os_ref.dtype)
        else:
            o_refs[0][...] = y.astype(o_refs[0].dtype)

    _modulated(i, npt, tm, ts, prow, (sc_ref, sh_ref), emit)


def _normmod(x, g3, mods, l, k_sc, k_sh, cfg, out_dtype, split_out=False):
    tp, d = x.shape
    tm = cfg["tm_row"]
    npt = cfg["T"] // tm
    nt = tp // tm
    kern = functools.partial(_normmod_kernel, npt=npt, tm=tm, ts=cfg["Ts"], prow=cfg["prow"],
                             split_out=split_out)
    if split_out:
        out_shape = (jax.ShapeDtypeStruct((cfg["T"], d), out_dtype),
                     jax.ShapeDtypeStruct((tp - cfg["T"], d), out_dtype))
        out_specs = (pl.BlockSpec((tm, d), lambda i: (jnp.minimum(i, npt - 1), 0)),
                     pl.BlockSpec((tm, d), lambda i: (jnp.maximum(i - npt, 0), 0)))
    else:
        out_shape = jax.ShapeDtypeStruct((tp, d), out_dtype)
        out_specs = pl.BlockSpec((tm, d), lambda i: (i, 0))
    return pl.pallas_call(
        kern,
        grid=(nt,),
        in_specs=[
            pl.BlockSpec((tm, d), lambda i: (i, 0)),
            pl.BlockSpec((None, 1, d), lambda i: (l, 0, 0)),
            pl.BlockSpec((None, MODROWS, d), lambda i: (l, 0, k_sc)),
            pl.BlockSpec((None, MODROWS, d), lambda i: (l, 0, k_sh)),
        ],
        out_specs=out_specs,
        out_shape=out_shape,
        compiler_params=_cparams(("arbitrary",), 40),
    )(x, g3, mods, mods)


def _mm_kernel(x_ref, w_ref, *o_refs):
    a = x_ref[...]
    if a.dtype != BF16:
        a = a.astype(BF16)
    acc = _dot(a, w_ref[...].astype(BF16))
    for o in o_refs:
        o[...] = acc.astype(o.dtype)


def _mm(x, w, l, n, out_dtypes, tm, tn, vmem=48, row_block0=0, rows=None):
    k = x.shape[-1]
    rows = x.shape[-2] if rows is None else rows
    if x.ndim == 3:
        x_spec = pl.BlockSpec((None, tm, k), lambda i, j: (l, i + row_block0, 0))
    else:
        x_spec = pl.BlockSpec((tm, k), lambda i, j: (i + row_block0, 0))
    outs = tuple(jax.ShapeDtypeStruct((rows, n), dt) for dt in out_dtypes)
    res = pl.pallas_call(
        _mm_kernel,
        grid=(rows // tm, n // tn),
        in_specs=[x_spec, pl.BlockSpec((None, k, tn), lambda i, j: (l, 0, j))],
        out_specs=tuple(pl.BlockSpec((tm, tn), lambda i, j: (i, j)) for _ in out_dtypes),
        out_shape=outs,
        compiler_params=_cparams(("parallel", "arbitrary"), vmem),
    )(x, w)
    return res


def _rope128(x, cos, sin, half):
    lane = lax.broadcasted_iota(I32, x.shape, 1)
    partner = jnp.where(lane < half, pltpu.roll(x, LANES - half, 1), pltpu.roll(x, half, 1))
    return x * cos + partner * sin


def _tailpost_kernel(t_ref, gq_ref, gkv_ref, bf_ref, cos_ref, sin_ref,
                     cq_ref, ckv_ref, ckvb_ref, kr_ref, lf_ref, *, qr, kvr, rope, hf):
    t = t_ref[...]
    cq_ref[...] = (_rms(t[:, :qr]) * gq_ref[...]).astype(BF16)
    ckv = _rms(t[:, qr:qr + kvr]) * gkv_ref[...]
    ckv_ref[...] = ckv
    ckvb_ref[...] = ckv.astype(BF16)
    kr = _rope128(t[:, qr + kvr:qr + kvr + LANES], cos_ref[...], sin_ref[...], rope // 2)
    kr_ref[...] = kr[:, :rope]
    z = t[:, qr + kvr + LANES:qr + kvr + LANES + hf] + bf_ref[...]
    lf_ref[...] = jnp.minimum(z, 0.0) - jnp.log1p(jnp.exp(-jnp.abs(z)))


def _tailpost(tail, gq3, gkv3, bf3, cos, sin, l, cfg):
    tp, ntail = tail.shape
    tm = cfg["tm_row"]
    qr, kvr, rope, hf = cfg["QR"], cfg["KVR"], cfg["ROPE"], cfg["HF"]
    kern = functools.partial(_tailpost_kernel, qr=qr, kvr=kvr, rope=rope, hf=hf)
    return pl.pallas_call(
        kern,
        grid=(tp // tm,),
        in_specs=[
            pl.BlockSpec((tm, ntail), lambda i: (i, 0)),
            pl.BlockSpec((None, 1, qr), lambda i: (l, 0, 0)),
            pl.BlockSpec((None, 1, kvr), lambda i: (l, 0, 0)),
            pl.BlockSpec((None, 1, hf), lambda i: (l, 0, 0)),
            pl.BlockSpec((tm, LANES), lambda i: (i, 0)),
            pl.BlockSpec((tm, LANES), lambda i: (i, 0)),
        ],
        out_specs=(
            pl.BlockSpec((tm, qr), lambda i: (i, 0)),
            pl.BlockSpec((tm, kvr), lambda i: (i, 0)),
            pl.BlockSpec((tm, kvr), lambda i: (i, 0)),
            pl.BlockSpec((tm, rope), lambda i: (i, 0)),
            pl.BlockSpec((tm, hf), lambda i: (i, 0)),
        ),
        out_shape=(
            jax.ShapeDtypeStruct((tp, qr), BF16),
            jax.ShapeDtypeStruct((tp, kvr), F32),
            jax.ShapeDtypeStruct((tp, kvr), BF16),
            jax.ShapeDtypeStruct((tp, rope), F32),
            jax.ShapeDtypeStruct((tp, hf), F32),
        ),
        compiler_params=_cparams(("parallel",), 32),
    )(tail, gq3, gkv3, bf3, cos, sin)


def _qcat_kernel(x_ref, w_ref, cos_ref, sin_ref, o_ref, *, half):
    acc = _dot(x_ref[...], w_ref[...])
    for g in range(acc.shape[1] // LANES):
        slab = acc[:, g * LANES:(g + 1) * LANES]
        if g % 2 == 1:
            slab = _rope128(slab, cos_ref[...], sin_ref[...], half)
        o_ref[:, g * LANES:(g + 1) * LANES] = slab.astype(o_ref.dtype)


def _qcat(cqn, w_uq_cat, cos, sin, l, cfg):
    tp, qr = cqn.shape
    n = w_uq_cat.shape[-1]
    tm = cfg["tm_mm"]
    tn = _pick_tile(n, 512, 2 * LANES)
    kern = functools.partial(_qcat_kernel, half=cfg["ROPE"] // 2)
    return pl.pallas_call(
        kern,
        grid=(tp // tm, n // tn),
        in_specs=[
            pl.BlockSpec((tm, qr), lambda i, j: (i, 0)),
            pl.BlockSpec((None, qr, tn), lambda i, j: (l, 0, j)),
            pl.BlockSpec((tm, LANES), lambda i, j: (i, 0)),
            pl.BlockSpec((tm, LANES), lambda i, j: (i, 0)),
        ],
        out_specs=pl.BlockSpec((tm, tn), lambda i, j: (i, j)),
        out_shape=jax.ShapeDtypeStruct((tp, n), BF16),
        compiler_params=_cparams(("parallel", "arbitrary"), 40),
    )(cqn, w_uq_cat, cos, sin)


def _kcat_kernel(x_ref, kr_ref, wk_ref, wv_ref, k_ref, v_ref, *, rope):
    x = x_ref[...]
    if x.dtype != BF16:
        x = x.astype(BF16)
    kn = _dot(x, wk_ref[...])
    v_ref[...] = _dot(x, wv_ref[...]).astype(v_ref.dtype)
    tm = x.shape[0]
    kr = jnp.concatenate([kr_ref[...], jnp.zeros((tm, LANES - rope), F32)], axis=1).astype(BF16)
    for h in range(kn.shape[1] // LANES):
        k_ref[:, (2 * h) * LANES:(2 * h + 1) * LANES] = kn[:, h * LANES:(h + 1) * LANES].astype(BF16)
        k_ref[:, (2 * h + 1) * LANES:(2 * h + 2) * LANES] = kr


def _kcat(ckv, kr, wk, wv, l, cfg, tm, layered):
    if layered:
        rows = ckv.shape[1]
        x_spec = pl.BlockSpec((None, tm, cfg["KVR"]), lambda i, j: (l, i, 0))
        kr_spec = pl.BlockSpec((None, tm, cfg["ROPE"]), lambda i, j: (l, i, 0))
    else:
        rows = ckv.shape[0]
        x_spec = pl.BlockSpec((tm, cfg["KVR"]), lambda i, j: (i, 0))
        kr_spec = pl.BlockSpec((tm, cfg["ROPE"]), lambda i, j: (i, 0))
    hm = cfg["HM"]
    kern = functools.partial(_kcat_kernel, rope=cfg["ROPE"])
    return pl.pallas_call(
        kern,
        grid=(rows // tm, hm // 2),
        in_specs=[
            x_spec, kr_spec,
            pl.BlockSpec((None, cfg["KVR"], 2 * LANES), lambda i, j: (l, 0, j)),
            pl.BlockSpec((None, cfg["KVR"], 2 * LANES), lambda i, j: (l, 0, j)),
        ],
        out_specs=(pl.BlockSpec((tm, 4 * LANES), lambda i, j: (i, j)),
                   pl.BlockSpec((tm, 2 * LANES), lambda i, j: (i, j))),
        out_shape=(jax.ShapeDtypeStruct((rows, hm * 2 * LANES), BF16),
                   jax.ShapeDtypeStruct((rows, hm * LANES), BF16)),
        compiler_params=_cparams(("parallel", "arbitrary"), 40),
    )(ckv, kr, wk, wv)


def _cumsum_kernel(x_ref, o_ref, carry_ref):
    j = pl.program_id(1)

    @pl.when(j == 0)
    def _():
        carry_ref[...] = jnp.zeros_like(carry_ref)

    x = x_ref[...]
    tb = x.shape[1]
    r = lax.broadcasted_iota(I32, (tb, tb), 0)
    c = lax.broadcasted_iota(I32, (tb, tb), 1)
    upper = jnp.where(r <= c, 1.0, 0.0).astype(BF16)
    x1 = x.astype(BF16)
    r1 = x - x1.astype(F32)
    x2 = r1.astype(BF16)
    x3 = (r1 - x2.astype(F32)).astype(BF16)
    cum = _dot(x1, upper) + _dot(x2, upper) + _dot(x3, upper) + carry_ref[...]
    o_ref[...] = cum
    carry_ref[...] = cum[:, tb - 1:tb]


def _cumsum(x):
    r, h, tl = x.shape
    tb = _pick_tile(tl, 512, LANES)
    return pl.pallas_call(
        _cumsum_kernel,
        grid=(r, tl // tb),
        in_specs=[pl.BlockSpec((None, h, tb), lambda i, j: (i, 0, j))],
        out_specs=pl.BlockSpec((None, h, tb), lambda i, j: (i, 0, j)),
        out_shape=jax.ShapeDtypeStruct((r, h, tl), F32),
        scratch_shapes=[pltpu.VMEM((h, 1), F32)],
        compiler_params=_cparams(("parallel", "arbitrary"), 32),
    )(x)


def _sb_chunk(q, k_c, v_c, carry, acc, before, upper, scale):
    tq = q.shape[0]
    z = _dot_nt(q, k_c) * scale
    sp = _softplus(z)
    log1m = -sp if before is None else jnp.where(before, -sp, 0.0)
    hi, lo = _split2(log1m)
    cs = _dot(jnp.concatenate([hi, lo], axis=0), upper)
    suffix = cs[:tq] + cs[tq:] + carry
    w = jnp.exp(z - sp + suffix)
    if before is not None:
        w = jnp.where(before, w, 0.0)
    acc = acc + _dot(w.astype(BF16), v_c)
    carry = carry + jnp.sum(log1m, axis=1, keepdims=True)
    return carry, acc


def _strict_upper():
    r = lax.broadcasted_iota(I32, (LANES, LANES), 0)
    c = lax.broadcasted_iota(I32, (LANES, LANES), 1)
    return jnp.where(r > c, 1.0, 0.0).astype(BF16)


def _sb_prompt_kernel(q_ref, k_ref, v_ref, oin_ref, o_ref, acc_ref, car_ref, *, tq, scale):
    del oin_ref
    qi = pl.program_id(1)
    q = q_ref[...]
    upper = _strict_upper()
    nch = tq // LANES
    acc_ref[...] = jnp.zeros_like(acc_ref)
    car_ref[...] = jnp.zeros_like(car_ref)
    row = lax.broadcasted_iota(I32, (tq, LANES), 0)
    col = lax.broadcasted_iota(I32, (tq, LANES), 1)

    def visit(base, c, masked):
        off = pl.multiple_of(base + c * LANES, LANES)
        before = (col + c * LANES < row) if masked else None
        carry, acc = _sb_chunk(q, k_ref[pl.ds(off, LANES), :], v_ref[pl.ds(off, LANES), :],
                               car_ref[...], acc_ref[...], before, upper, scale)
        car_ref[...] = carry
        acc_ref[...] = acc

    for c in reversed(range(nch)):
        visit(qi * tq, c, True)

    def cond(s):
        kb, dead = s
        return jnp.logical_and(kb >= 0, dead == 0)

    def body(s):
        kb, _ = s
        for c in reversed(range(nch)):
            visit(kb * tq, c, False)
        dead = (jnp.max(car_ref[...]) < SB_DEAD).astype(I32)
        return kb - 1, dead

    lax.while_loop(cond, body, (qi - 1, jnp.int32(0)))
    o_ref[...] = acc_ref[...].astype(o_ref.dtype)


def _pad_rows(x, n):
    return jnp.concatenate([x, jnp.zeros((n - x.shape[0], x.shape[1]), x.dtype)], axis=0)


def _sb_dec_kernel(q_ref, kn_ref, vn_ref, kc_ref, vc_ref, oin_ref, o_ref, *, ts, p, scale):
    del oin_ref
    q = q_ref[...]
    upper = _strict_upper()
    row = lax.broadcasted_iota(I32, (ts, LANES), 0)
    col = lax.broadcasted_iota(I32, (ts, LANES), 1)
    carry = jnp.zeros((ts, 1), F32)
    acc = jnp.zeros((ts, HEAD_DIM), F32)
    carry, acc = _sb_chunk(q, _pad_rows(kn_ref[...], LANES), _pad_rows(vn_ref[...], LANES),
                           carry, acc, col < row, upper, scale)
    for c in reversed(range(p // LANES)):
        k_c = kc_ref[c * LANES:(c + 1) * LANES, :].astype(BF16)
        v_c = vc_ref[c * LANES:(c + 1) * LANES, :].astype(BF16)
        carry, acc = _sb_chunk(q, k_c, v_c, carry, acc, None, upper, scale)
    o_ref[...] = acc.astype(o_ref.dtype)


def _flash_step(s, v_b, m_ref, l_ref, acc_ref):
    m_prev = m_ref[...]
    m_new = jnp.maximum(m_prev, jnp.max(s, axis=1, keepdims=True))
    alpha = jnp.exp(m_prev - m_new)
    p = jnp.exp(s - m_new)
    l_ref[...] = alpha * l_ref[...] + jnp.sum(p, axis=1, keepdims=True)
    acc_ref[...] = alpha * acc_ref[...] + _dot(p.astype(BF16), v_b)
    m_ref[...] = m_new


def _chunk_id(x):
    return lax.shift_right_logical(x, int(math.log2(CHUNK)))


def _sm_prompt_kernel(*refs, tq, scale, fox):
    if fox:
        q_ref, k_ref, v_ref, cq_ref, ck_ref, oin_ref, o_ref, m_ref, l_ref, acc_ref = refs
    else:
        q_ref, k_ref, v_ref, oin_ref, o_ref, m_ref, l_ref, acc_ref = refs
    del oin_ref
    qi = pl.program_id(1)
    q = q_ref[...]
    m_ref[...] = jnp.full_like(m_ref, NEG_BIG)
    l_ref[...] = jnp.zeros_like(l_ref)
    acc_ref[...] = jnp.zeros_like(acc_ref)

    def block(kb, masked):
        off = pl.multiple_of(kb * tq, tq)
        s = _dot_nt(q, k_ref[pl.ds(off, tq), :]) * scale
        if fox:
            s = s + (cq_ref[...] - ck_ref[:, pl.ds(off, tq)])
        if masked:
            r = lax.broadcasted_iota(I32, (tq, tq), 0)
            c = lax.broadcasted_iota(I32, (tq, tq), 1)
            valid = (c <= r) if fox else (_chunk_id(c) <= _chunk_id(r))
            s = jnp.where(valid, s, NEG_BIG)
        _flash_step(s, v_ref[pl.ds(off, tq), :], m_ref, l_ref, acc_ref)

    def body(kb, carry):
        block(kb, False)
        return carry

    lax.fori_loop(0, qi, body, 0)
    block(qi, True)
    o_ref[...] = (acc_ref[...] / l_ref[...]).astype(o_ref.dtype)


def _sm_dec_kernel(*refs, ts, p, scale, fox, qpos0):
    if fox:
        q_ref, kn_ref, vn_ref, kc_ref, vc_ref, cq_ref, ck_ref, oin_ref, o_ref = refs
    else:
        q_ref, kn_ref, vn_ref, kc_ref, vc_ref, oin_ref, o_ref = refs
    del oin_ref
    q = q_ref[...]
    kc = kc_ref[...]
    vc = vc_ref[...]
    if kc.dtype != BF16:
        kc = kc.astype(BF16)
        vc = vc.astype(BF16)
    s_c = _dot_nt(q, kc) * scale
    s_n = _dot_nt(q, _pad_rows(kn_ref[...], LANES)) * scale
    r = lax.broadcasted_iota(I32, (ts, LANES), 0)
    c = lax.broadcasted_iota(I32, (ts, LANES), 1)
    if fox:
        ck = ck_ref[...]
        cq = cq_ref[...]
        s_c = s_c + (cq - ck[:, :p])
        s_n = s_n + (cq - ck[:, p:p + LANES])
        valid = c <= r
    else:
        valid = jnp.logical_and(_chunk_id(c + qpos0) <= _chunk_id(r + qpos0), c < ts)
    s_n = jnp.where(valid, s_n, NEG_BIG)
    m = jnp.maximum(jnp.max(s_c, axis=1, keepdims=True), jnp.max(s_n, axis=1, keepdims=True))
    p_c = jnp.exp(s_c - m)
    p_n = jnp.exp(s_n - m)
    den = jnp.sum(p_c, axis=1, keepdims=True) + jnp.sum(p_n, axis=1, keepdims=True)
    num = _dot(p_c.astype(BF16), vc) + _dot(p_n.astype(BF16), _pad_rows(vn_ref[...], LANES))
    o_ref[...] = (num / den).astype(o_ref.dtype)


def _o_specs(o_prev, tp, dmix):
    shape = jax.ShapeDtypeStruct((tp, dmix), BF16)
    if o_prev is None:
        o_prev = jnp.zeros((8, LANES), BF16)
        return o_prev, shape, {}
    return o_prev, shape, None


def _attn_call(kern, grid, in_specs, args, o_prev, o_spec, tp, dmix, scratch, vmem, n_in):
    o_arg, o_shape, alias = _o_specs(o_prev, tp, dmix)
    aliases = {} if alias is not None else {n_in: 0}
    return pl.pallas_call(
        kern,
        grid=grid,
        in_specs=in_specs + [pl.BlockSpec(memory_space=pl.ANY)],
        out_specs=o_spec,
        out_shape=o_shape,
        scratch_shapes=scratch,
        input_output_aliases=aliases,
        compiler_params=_cparams(("parallel", "arbitrary"), vmem),
    )(*args, o_arg)


def _sb_prompt(qkv, o_prev, cfg, cq0, ck0, cv0, oc0):
    t, tp, hs = cfg["T"], cfg["Tp"], cfg["HS"]
    tq = cfg["tq_sb"]
    kern = functools.partial(_sb_prompt_kernel, tq=tq, scale=1.0 / math.sqrt(HEAD_DIM))
    in_specs = [
        pl.BlockSpec((tq, LANES), lambda h, i: (i, cq0 + h)),
        pl.BlockSpec((t, LANES), lambda h, i: (0, ck0 + h)),
        pl.BlockSpec((t, LANES), lambda h, i: (0, cv0 + h)),
    ]
    o_spec = pl.BlockSpec((tq, LANES), lambda h, i: (i, oc0 + h))
    scratch = [pltpu.VMEM((tq, HEAD_DIM), F32), pltpu.VMEM((tq, 1), F32)]
    return _attn_call(kern, (hs, t // tq), in_specs, (qkv, qkv, qkv), o_prev, o_spec, tp, cfg["DMIX"],
                      scratch, 40, 3)


def _sb_dec(qkv, kc, vc, l, o_prev, cfg, cq0, ck0, cv0, oc0):
    t, tp, ts, bs, p, hs = cfg["T"], cfg["Tp"], cfg["Ts"], cfg["Bs"], cfg["P"], cfg["HS"]
    rb = t // ts
    kern = functools.partial(_sb_dec_kernel, ts=ts, p=p, scale=1.0 / math.sqrt(HEAD_DIM))
    in_specs = [
        pl.BlockSpec((ts, LANES), lambda b, h: (rb + b, cq0 + h)),
        pl.BlockSpec((ts, LANES), lambda b, h: (rb + b, ck0 + h)),
        pl.BlockSpec((ts, LANES), lambda b, h: (rb + b, cv0 + h)),
        pl.BlockSpec((None, None, p, LANES), lambda b, h: (l, b, 0, h)),
        pl.BlockSpec((None, None, p, LANES), lambda b, h: (l, b, 0, h)),
    ]
    o_spec = pl.BlockSpec((ts, LANES), lambda b, h: (rb + b, oc0 + h))
    return _attn_call(kern, (bs, hs), in_specs, (qkv, qkv, qkv, kc, vc), o_prev, o_spec, tp, cfg["DMIX"],
                      [], 32, 5)


def _fox_prompt(qkv, cum_row, cum_col, o_prev, cfg, cq0, ck0, cv0, oc0):
    t, tp, hf = cfg["T"], cfg["Tp"], cfg["HF"]
    tq = cfg["tq_sm"]
    kern = functools.partial(_sm_prompt_kernel, tq=tq, scale=1.0 / math.sqrt(HEAD_DIM), fox=True)
    in_specs = [
        pl.BlockSpec((tq, LANES), lambda h, i: (i, cq0 + h)),
        pl.BlockSpec((t, LANES), lambda h, i: (0, ck0 + h)),
        pl.BlockSpec((t, LANES), lambda h, i: (0, cv0 + h)),
        pl.BlockSpec((None, tq, 1), lambda h, i: (h, i, 0)),
        pl.BlockSpec((None, 1, t), lambda h, i: (h, 0, 0)),
    ]
    o_spec = pl.BlockSpec((tq, LANES), lambda h, i: (i, oc0 + h))
    scratch = [pltpu.VMEM((tq, 1), F32), pltpu.VMEM((tq, 1), F32), pltpu.VMEM((tq, HEAD_DIM), F32)]
    return _attn_call(kern, (hf, t // tq), in_specs, (qkv, qkv, qkv, cum_col, cum_row), o_prev, o_spec,
                      tp, cfg["DMIX"], scratch, 40, 5)


def _fox_dec(qkv, kc, vc, cq_s, ck_s, l, o_prev, cfg, cq0, ck0, cv0, oc0):
    t, tp, ts, bs, p, hf = cfg["T"], cfg["Tp"], cfg["Ts"], cfg["Bs"], cfg["P"], cfg["HF"]
    rb = t // ts
    kern = functools.partial(_sm_dec_kernel, ts=ts, p=p, scale=1.0 / math.sqrt(HEAD_DIM), fox=True, qpos0=p)
    in_specs = [
        pl.BlockSpec((ts, LANES), lambda b, h: (rb + b, cq0 + h)),
        pl.BlockSpec((ts, LANES), lambda b, h: (rb + b, ck0 + h)),
        pl.BlockSpec((ts, LANES), lambda b, h: (rb + b, cv0 + h)),
        pl.BlockSpec((None, None, p, LANES), lambda b, h: (l, b, 0, h)),
        pl.BlockSpec((None, None, p, LANES), lambda b, h: (l, b, 0, h)),
        pl.BlockSpec((None, None, ts, 1), lambda b, h: (b, h, 0, 0)),
        pl.BlockSpec((None, None, 1, p + LANES), lambda b, h: (b, h, 0, 0)),
    ]
    o_spec = pl.BlockSpec((ts, LANES), lambda b, h: (rb + b, oc0 + h))
    return _attn_call(kern, (bs, hf), in_specs, (qkv, qkv, qkv, kc, vc, cq_s, ck_s), o_prev, o_spec, tp,
                      cfg["DMIX"], [], 32, 7)


def _mla_scale(cfg):
    return 1.0 / math.sqrt(cfg["NOPE"] + cfg["ROPE"])


def _mla_prompt(qcat, kcat, vm, o_prev, cfg, oc0):
    t, tp, hm = cfg["T"], cfg["Tp"], cfg["HM"]
    tq = cfg["tq_sm"]
    kern = functools.partial(_sm_prompt_kernel, tq=tq, scale=_mla_scale(cfg), fox=False)
    in_specs = [
        pl.BlockSpec((tq, 2 * LANES), lambda h, i: (i, h)),
        pl.BlockSpec((t, 2 * LANES), lambda h, i: (0, h)),
        pl.BlockSpec((t, LANES), lambda h, i: (0, h)),
    ]
    o_spec = pl.BlockSpec((tq, LANES), lambda h, i: (i, oc0 + h))
    scratch = [pltpu.VMEM((tq, 1), F32), pltpu.VMEM((tq, 1), F32), pltpu.VMEM((tq, HEAD_DIM), F32)]
    return _attn_call(kern, (hm, t // tq), in_specs, (qcat, kcat, vm), o_prev, o_spec, tp, cfg["DMIX"],
                      scratch, 48, 3)


def _mla_dec(qcat, kcat, vm, kcat_c, vm_c, o_prev, cfg, oc0):
    t, tp, ts, bs, p, hm = cfg["T"], cfg["Tp"], cfg["Ts"], cfg["Bs"], cfg["P"], cfg["HM"]
    rb = t // ts
    kern = functools.partial(_sm_dec_kernel, ts=ts, p=p, scale=_mla_scale(cfg), fox=False, qpos0=p)
    in_specs = [
        pl.BlockSpec((ts, 2 * LANES), lambda b, h: (rb + b, h)),
        pl.BlockSpec((ts, 2 * LANES), lambda b, h: (rb + b, h)),
        pl.BlockSpec((ts, LANES), lambda b, h: (rb + b, h)),
        pl.BlockSpec((p, 2 * LANES), lambda b, h: (b, h)),
        pl.BlockSpec((p, LANES), lambda b, h: (b, h)),
    ]
    o_spec = pl.BlockSpec((ts, LANES), lambda b, h: (rb + b, oc0 + h))
    return _attn_call(kern, (bs, hm), in_specs, (qcat, kcat, vm, kcat_c, vm_c), o_prev, o_spec, tp,
                      cfg["DMIX"], [], 32, 5)


def _gnwo_kernel(o_ref, g_ref, w_ref, x_ref, gt_ref, yin_ref, y_ref, on_ref, *, bounds, sample, nb, ts, prow):
    del yin_ref
    j = pl.program_id(1)

    @pl.when(j == 0)
    def _():
        for lo, hi in bounds:
            part = o_ref[:, lo:hi].astype(F32)
            on_ref[:, lo:hi] = (_rms(part) * g_ref[:, lo:hi]).astype(BF16)

    mix = _dot(on_ref[...], w_ref[...].astype(BF16))
    gt = _expand_rows(gt_ref[0:nb, :], ts) if sample else gt_ref[prow:prow + 1, :]
    y_ref[...] = x_ref[...] + gt * mix


def _gnwo_part(o, g3, w_o, x, mods, l, cfg, sample, y_prev):
    tp, dmix = o.shape
    d = x.shape[1]
    t, bs, ts = cfg["T"], cfg["Bs"], cfg["Ts"]
    tm = bs * ts if sample else _pick_tile(t, 1024, 16)
    rb0 = t // tm if sample else 0
    rows = bs * ts if sample else t
    tn = _pick_tile(d, 256, LANES)
    bounds = ((0, cfg["DSB"]), (cfg["DSB"], cfg["DSB"] + cfg["DFX"]), (cfg["DSB"] + cfg["DFX"], dmix))
    kern = functools.partial(_gnwo_kernel, bounds=bounds, sample=sample, nb=bs, ts=ts, prow=cfg["prow"])
    if y_prev is None:
        y_prev, aliases = jnp.zeros((8, LANES), F32), {}
    else:
        aliases = {5: 0}
    return pl.pallas_call(
        kern,
        grid=(rows // tm, d // tn),
        in_specs=[
            pl.BlockSpec((tm, dmix), lambda i, j: (i + rb0, 0)),
            pl.BlockSpec((None, 1, dmix), lambda i, j: (l, 0, 0)),
            pl.BlockSpec((None, dmix, tn), lambda i, j: (l, 0, j)),
            pl.BlockSpec((tm, tn), lambda i, j: (i + rb0, j)),
            pl.BlockSpec((None, MODROWS, tn), lambda i, j: (l, 0, 2 * (d // tn) + j)),
            pl.BlockSpec(memory_space=pl.ANY),
        ],
        out_specs=pl.BlockSpec((tm, tn), lambda i, j: (i + rb0, j)),
        out_shape=jax.ShapeDtypeStruct((tp, d), F32),
        scratch_shapes=[pltpu.VMEM((tm, dmix), BF16)],
        input_output_aliases=aliases,
        compiler_params=_cparams(("parallel", "arbitrary"), 48),
    )(o, g3, w_o, x, mods, y_prev)


def _gnwo(o, g3, w_o, x, mods, l, cfg):
    y = _gnwo_part(o, g3, w_o, x, mods, l, cfg, False, None)
    return _gnwo_part(o, g3, w_o, x, mods, l, cfg, True, y)


def _pack_pair(a, b):
    ua = lax.bitcast_convert_type(a.astype(BF16).astype(F32), U32)
    ub = lax.bitcast_convert_type(b.astype(BF16).astype(F32), U32)
    return (ua & jnp.uint32(0xFFFF0000)) | lax.shift_right_logical(ub, jnp.uint32(16))


def _unpack_pair(u):
    hi = lax.bitcast_convert_type(u & jnp.uint32(0xFFFF0000), F32)
    lo = lax.bitcast_convert_type(lax.shift_left(u, jnp.uint32(16)), F32)
    return hi, lo


def _moepre_kernel(x_ref, g_ref, sc_ref, sh_ref, wr_ref, br_ref,
                   hp_ref, idx_ref, gate_ref, pos_ref, cnt_ref, carry_ref, *, npt, tm, ts, prow):
    i = pl.program_id(0)

    @pl.when(i == 0)
    def _():
        carry_ref[...] = jnp.zeros_like(carry_ref)

    xn = _rms(x_ref[...]) * g_ref[...]
    d = xn.shape[1]

    def emit(sc, sh):
        h = xn * (1.0 + sc) + sh
        hp_ref[...] = _pack_pair(h[:, :d // 2], h[:, d // 2:])
        h1, h2 = _split2(h)
        w = wr_ref[...]
        w1, w2 = _split2(w)
        logits = _dot(h1, w1) + (_dot(h1, w2) + _dot(h2, w1)) + br_ref[...]
        lane = lax.broadcasted_iota(I32, (tm, LANES), 1).astype(F32)
        picks, vals, sels = [], [], []
        work = logits
        for _ in range(TOP_K):
            m = jnp.max(work, axis=1, keepdims=True)
            sel = jnp.min(jnp.where(work == m, lane, float(LANES)), axis=1, keepdims=True)
            hit = lane == sel
            picks.append(hit)
            vals.append(m)
            sels.append(sel)
            work = jnp.where(hit, -jnp.inf, work)
        es = [jnp.exp(v - vals[0]) for v in vals]
        den = es[0] + es[1] + es[2] + es[3]
        onehot = jnp.zeros((tm, LANES), F32)
        for hit in picks:
            onehot = onehot + jnp.where(hit, 1.0, 0.0)
        r = lax.broadcasted_iota(I32, (tm, tm), 0)
        c = lax.broadcasted_iota(I32, (tm, tm), 1)
        lower = jnp.where(c < r, 1.0, 0.0).astype(BF16)
        rank = _dot(lower, onehot.astype(BF16)) + carry_ref[...]
        for k in range(TOP_K):
            idx_ref[:, k:k + 1] = sels[k].astype(I32)
            gate_ref[:, k:k + 1] = es[k] / den
            pos_ref[:, k:k + 1] = jnp.sum(jnp.where(picks[k], rank, 0.0), axis=1,
                                          keepdims=True).astype(I32)
        total = carry_ref[...] + jnp.sum(onehot, axis=0, keepdims=True)
        carry_ref[...] = total
        cnt_ref[...] = total.astype(I32)

    _modulated(i, npt, tm, ts, prow, (sc_ref, sh_ref), emit)


def _moepre(x, g3, mods, w_router, b3, l, cfg):
    tp, d = x.shape
    tm = cfg["tm_row"]
    kern = functools.partial(_moepre_kernel, npt=cfg["T"] // tm, tm=tm, ts=cfg["Ts"], prow=cfg["prow"])
    return pl.pallas_call(
        kern,
        grid=(tp // tm,),
        in_specs=[
            pl.BlockSpec((tm, d), lambda i: (i, 0)),
            pl.BlockSpec((None, 1, d), lambda i: (l, 0, 0)),
            pl.BlockSpec((None, MODROWS, d), lambda i: (l, 0, 4)),
            pl.BlockSpec((None, MODROWS, d), lambda i: (l, 0, 3)),
            pl.BlockSpec((None, d, LANES), lambda i: (l, 0, 0)),
            pl.BlockSpec((None, 1, LANES), lambda i: (l, 0, 0)),
        ],
        out_specs=(
            pl.BlockSpec((tm, d // 2), lambda i: (i, 0)),
            pl.BlockSpec((tm, TOP_K), lambda i: (i, 0)),
            pl.BlockSpec((tm, TOP_K), lambda i: (i, 0)),
            pl.BlockSpec((tm, TOP_K), lambda i: (i, 0)),
            pl.BlockSpec((1, LANES), lambda i: (0, 0)),
        ),
        out_shape=(
            jax.ShapeDtypeStruct((tp, d // 2), U32),
            jax.ShapeDtypeStruct((tp, TOP_K), I32),
            jax.ShapeDtypeStruct((tp, TOP_K), F32),
            jax.ShapeDtypeStruct((tp, TOP_K), I32),
            jax.ShapeDtypeStruct((1, LANES), I32),
        ),
        scratch_shapes=[pltpu.VMEM((1, LANES), F32)],
        compiler_params=_cparams(("arbitrary",), 40),
    )(x, g3, mods, mods, w_router, b3)


def _slot_token(a):
    return lax.shift_right_logical(a, int(math.log2(TOP_K)))


def _slot_choice(a):
    return lax.bitwise_and(a, TOP_K - 1)


def _dispatch_kernel(zrow_ref, dest_ref, hp_ref, xs_ref, zero_ref, zsem, sem, *, td, ne):
    i = pl.program_id(0)

    def zero_copy(e):
        r0 = pl.multiple_of(zrow_ref[e], MOE_TILE)
        return pltpu.make_async_copy(zero_ref, xs_ref.at[pl.ds(r0, MOE_TILE)], zsem)

    @pl.when(i == 0)
    def _():
        zero_ref[...] = jnp.zeros_like(zero_ref)
        for e in range(ne):
            zero_copy(e).start()
        for e in range(ne):
            zero_copy(e).wait()

    n = td * TOP_K

    def row_copy(a):
        t = i * td + _slot_token(a)
        return pltpu.make_async_copy(hp_ref.at[pl.ds(t, 1)], xs_ref.at[pl.ds(dest_ref[0, 0, a], 1)], sem)

    def start(a, c):
        row_copy(a).start()
        return c

    def wait(a, c):
        row_copy(a).wait()
        return c

    lax.fori_loop(0, n, start, 0)
    lax.fori_loop(0, n, wait, 0)


def _dispatch(hp, dest, zrow, cfg):
    tp, dh = hp.shape
    td = cfg["tm_row"]
    nt = tp // td
    kern = functools.partial(_dispatch_kernel, td=td, ne=cfg["E"])
    grid_spec = pltpu.PrefetchScalarGridSpec(
        num_scalar_prefetch=1,
        grid=(nt,),
        in_specs=[
            pl.BlockSpec((1, 1, td * TOP_K), lambda i, z: (i, 0, 0), memory_space=pltpu.SMEM),
            pl.BlockSpec(memory_space=pl.ANY),
        ],
        out_specs=pl.BlockSpec(memory_space=pl.ANY),
        scratch_shapes=[pltpu.VMEM((MOE_TILE, dh), U32), pltpu.SemaphoreType.DMA(()),
                        pltpu.SemaphoreType.DMA(())],
    )
    return pl.pallas_call(
        kern,
        grid_spec=grid_spec,
        out_shape=jax.ShapeDtypeStruct((cfg["PA"], dh), U32),
        compiler_params=pltpu.CompilerParams(dimension_semantics=("arbitrary",), vmem_limit_bytes=32 * MIB,
                                             has_side_effects=True),
    )(zrow, dest.reshape(nt, 1, td * TOP_K), hp)


def _tile_is_first(te_ref, i):
    prev = te_ref[jnp.maximum(i - 1, 0)]
    return jnp.logical_or(i == 0, te_ref[i] != prev)


def _g1_kernel(te_ref, nv_ref, xs_ref, wg_ref, bg_ref, wu_ref, bu_ref, act_ref, wgb_ref, wub_ref):
    i = pl.program_id(1)

    @pl.when(i < nv_ref[0])
    def _():
        @pl.when(_tile_is_first(te_ref, i))
        def _():
            wgb_ref[...] = wg_ref[...].astype(BF16)
            wub_ref[...] = wu_ref[...].astype(BF16)

        xa, xb = _unpack_pair(xs_ref[...])
        xa = xa.astype(BF16)
        xb = xb.astype(BF16)
        dh = xa.shape[1]
        g = _dot(xa, wgb_ref[:dh, :]) + _dot(xb, wgb_ref[dh:, :]) + bg_ref[...]
        u = _dot(xa, wub_ref[:dh, :]) + _dot(xb, wub_ref[dh:, :]) + bu_ref[...]
        g = jnp.minimum(g, SWIGLU_LIMIT)
        u = jnp.clip(u, -SWIGLU_LIMIT, SWIGLU_LIMIT)
        act_ref[...] = ((u + 1.0) * g * _sigmoid(SWIGLU_ALPHA * g)).astype(act_ref.dtype)


def _g1(xs, w_gate, b_gate4, w_up, b_up4, te, nv, l, cfg):
    pa, dh = xs.shape
    d, de = w_gate.shape[2], w_gate.shape[3]
    tn = _pick_tile(de, 512, LANES)
    nt = pa // MOE_TILE

    def row(i, te_ref, nv_ref):
        return jnp.minimum(i, nv_ref[0] - 1)

    grid_spec = pltpu.PrefetchScalarGridSpec(
        num_scalar_prefetch=2,
        grid=(de // tn, nt),
        in_specs=[
            pl.BlockSpec((MOE_TILE, dh), lambda j, i, te_ref, nv_ref: (row(i, te_ref, nv_ref), 0)),
            pl.BlockSpec((None, None, d, tn), lambda j, i, te_ref, nv_ref: (l, te_ref[i], 0, j)),
            pl.BlockSpec((None, None, 1, tn), lambda j, i, te_ref, nv_ref: (l, te_ref[i], 0, j)),
            pl.BlockSpec((None, None, d, tn), lambda j, i, te_ref, nv_ref: (l, te_ref[i], 0, j)),
            pl.BlockSpec((None, None, 1, tn), lambda j, i, te_ref, nv_ref: (l, te_ref[i], 0, j)),
        ],
        out_specs=pl.BlockSpec((MOE_TILE, tn), lambda j, i, te_ref, nv_ref: (row(i, te_ref, nv_ref), j)),
        scratch_shapes=[pltpu.VMEM((d, tn), BF16), pltpu.VMEM((d, tn), BF16)],
    )
    return pl.pallas_call(
        _g1_kernel,
        grid_spec=grid_spec,
        out_shape=jax.ShapeDtypeStruct((pa, de), BF16),
        compiler_params=_cparams(("arbitrary", "arbitrary"), 56),
    )(te, nv, xs, w_gate, b_gate4, w_up, b_up4)


def _g2_kernel(te_ref, nv_ref, act_ref, wd_ref, bd_ref, ys_ref, wdb_ref):
    i = pl.program_id(1)

    @pl.when(i < nv_ref[0])
    def _():
        @pl.when(_tile_is_first(te_ref, i))
        def _():
            wdb_ref[...] = wd_ref[...].astype(BF16)

        y = _dot(act_ref[...], wdb_ref[...]) + bd_ref[...]
        half = y.shape[1] // 2
        ys_ref[...] = _pack_pair(y[:, :half], y[:, half:])


def _g2(act, w_down, b_down4, te, nv, l, cfg):
    pa, de = act.shape
    d = w_down.shape[3]
    tn = _pick_tile(d, 2048, 2 * LANES)
    nt = pa // MOE_TILE

    def row(i, te_ref, nv_ref):
        return jnp.minimum(i, nv_ref[0] - 1)

    grid_spec = pltpu.PrefetchScalarGridSpec(
        num_scalar_prefetch=2,
        grid=(d // tn, nt),
        in_specs=[
            pl.BlockSpec((MOE_TILE, de), lambda j, i, te_ref, nv_ref: (row(i, te_ref, nv_ref), 0)),
            pl.BlockSpec((None, None, de, tn), lambda j, i, te_ref, nv_ref: (l, te_ref[i], 0, j)),
            pl.BlockSpec((None, None, 1, tn), lambda j, i, te_ref, nv_ref: (l, te_ref[i], 0, j)),
        ],
        out_specs=pl.BlockSpec((MOE_TILE, tn // 2), lambda j, i, te_ref, nv_ref: (row(i, te_ref, nv_ref), j)),
        scratch_shapes=[pltpu.VMEM((de, tn), BF16)],
    )
    return pl.pallas_call(
        _g2_kernel,
        grid_spec=grid_spec,
        out_shape=jax.ShapeDtypeStruct((pa, d // 2), U32),
        compiler_params=_cparams(("arbitrary", "arbitrary"), 48),
    )(te, nv, act, w_down, b_down4), tn


def _combine_kernel(dcur_ref, dnxt_ref, ys_ref, gate_ref, x_ref, gt_ref, o_ref, buf_ref, sem,
                    *, tc, nt, tn, npt, ts, prow):
    i = pl.program_id(0)
    slot = lax.rem(i, 2)
    n = tc * TOP_K

    def row_copy(d_ref, s, a):
        return pltpu.make_async_copy(ys_ref.at[pl.ds(d_ref[0, 0, a], 1)],
                                     buf_ref.at[s, _slot_choice(a), pl.ds(_slot_token(a), 1)], sem.at[s])

    def issue(d_ref, s):
        lax.fori_loop(0, n, lambda a, c: (row_copy(d_ref, s, a).start(), c)[1], 0)

    @pl.when(i == 0)
    def _():
        issue(dcur_ref, 0)

    @pl.when(i + 1 < nt)
    def _():
        issue(dnxt_ref, 1 - slot)

    lax.fori_loop(0, n, lambda a, c: (row_copy(dcur_ref, slot, a).wait(), c)[1], 0)

    gate = gate_ref[...]
    half = tn // 2
    dh = buf_ref.shape[-1]
    parts = [None] * (2 * (dh // half))
    for k in range(TOP_K):
        hi, lo = _unpack_pair(buf_ref[slot, k])
        gk = gate[:, k:k + 1]
        for j in range(dh // half):
            a = gk * hi[:, j * half:(j + 1) * half]
            b = gk * lo[:, j * half:(j + 1) * half]
            parts[2 * j] = a if parts[2 * j] is None else parts[2 * j] + a
            parts[2 * j + 1] = b if parts[2 * j + 1] is None else parts[2 * j + 1] + b
    moe = jnp.concatenate(parts, axis=1)

    def emit(gt):
        o_ref[...] = x_ref[...] + gt * moe

    _modulated(i, npt, tc, ts, prow, (gt_ref,), emit)


def _combine(ys, dest, gate, x, mods, l, tn, cfg):
    tp, d = x.shape
    tc = cfg["tc"]
    nt = tp // tc
    dh = ys.shape[1]
    kern = functools.partial(_combine_kernel, tc=tc, nt=nt, tn=tn, npt=cfg["T"] // tc, ts=cfg["Ts"],
                             prow=cfg["prow"])
    dest3 = dest.reshape(nt, 1, tc * TOP_K)
    return pl.pallas_call(
        kern,
        grid=(nt,),
        in_specs=[
            pl.BlockSpec((1, 1, tc * TOP_K), lambda i: (i, 0, 0), memory_space=pltpu.SMEM),
            pl.BlockSpec((1, 1, tc * TOP_K), lambda i: (jnp.minimum(i + 1, nt - 1), 0, 0),
                         memory_space=pltpu.SMEM),
            pl.BlockSpec(memory_space=pl.ANY),
            pl.BlockSpec((tc, TOP_K), lambda i: (i, 0)),
            pl.BlockSpec((tc, d), lambda i: (i, 0)),
            pl.BlockSpec((None, MODROWS, d), lambda i: (l, 0, 5)),
        ],
        out_specs=pl.BlockSpec((tc, d), lambda i: (i, 0)),
        out_shape=jax.ShapeDtypeStruct((tp, d), F32),
        scratch_shapes=[pltpu.VMEM((2, TOP_K, tc, dh), U32), pltpu.SemaphoreType.DMA((2,))],
        compiler_params=_cparams(("arbitrary",), 40),
    )(dest3, dest3, ys, gate, x, mods)


def _moe(x, g3, mods, w_router, b_router3, w_gate, b_gate4, w_up, b_up4, w_down, b_down4, l, cfg):
    ne = cfg["E"]
    hp, idx, gate, pos, cnt = _moepre(x, g3, mods, w_router, b_router3, l, cfg)
    cnt = cnt[0, :ne]
    pc = ((cnt + MOE_TILE - 1) // MOE_TILE) * MOE_TILE
    ends = jnp.cumsum(pc)
    off = ends - pc
    dest = off[idx] + pos
    nt = cfg["PA"] // MOE_TILE
    tile_ends = ends // MOE_TILE
    nv = tile_ends[-1]
    tiles = jnp.minimum(jnp.arange(nt, dtype=I32), nv - 1)
    te = jnp.minimum(jnp.searchsorted(tile_ends, tiles, side="right"), ne - 1).astype(I32)
    zrow = jnp.clip(ends - MOE_TILE, 0, cfg["PA"] - MOE_TILE).astype(I32)
    nv1 = nv.reshape(1).astype(I32)
    xs = _dispatch(hp, dest.astype(I32), zrow, cfg)
    act = _g1(xs, w_gate, b_gate4, w_up, b_up4, te, nv1, l, cfg)
    ys, tn = _g2(act, w_down, b_down4, te, nv1, l, cfg)
    return _combine(ys, dest.astype(I32), gate, x, mods, l, tn, cfg)


def _rope_tables(cfg):
    half = cfg["ROPE"] // 2
    inv_freq = ROPE_THETA ** (-jnp.arange(half, dtype=F32) / half)
    pos = jnp.concatenate([jnp.arange(cfg["T"], dtype=I32),
                           jnp.tile(cfg["P"] + jnp.arange(cfg["Ts"], dtype=I32), cfg["Bs"])])
    ang = pos.astype(F32)[:, None] * inv_freq[None, :]
    cos, sin = jnp.cos(ang), jnp.sin(ang)
    pad = jnp.zeros((pos.shape[0], LANES - 2 * half), F32)
    return (jnp.concatenate([cos, cos, pad], axis=1), jnp.concatenate([-sin, sin, pad], axis=1))


def kernel(x_prompt, x_sample, c_prompt, c_sample, cache_sb_k, cache_sb_v, cache_fox_k, cache_fox_v,
           cache_fox_logf, cache_mla_ckv, cache_mla_krope, w_in, b_forget, g_q, w_uq, g_kv, w_uk, w_uv,
           g_out, w_o, g_attn, g_mlp, w_ada, b_ada, w_router, b_router, w_gate, b_gate, w_up, b_up,
           w_down, b_down, g_final, w_final_ada, b_final_ada):
    nl, d, _ = w_in.shape
    bp, t, _ = x_prompt.shape
    bs, ts, _ = x_sample.shape
    p = cache_sb_k.shape[2]
    hs, hf = cache_sb_k.shape[3], cache_fox_k.shape[3]
    kvr, rope = cache_mla_ckv.shape[-1], cache_mla_krope.shape[-1]
    hm, nope, vdim = w_uk.shape[2], w_uk.shape[3], w_uv.shape[3]
    qr = g_q.shape[1]
    ne, de = w_router.shape[-1], w_gate.shape[-1]
    dsb, dfx = hs * HEAD_DIM, hf * HEAD_DIM
    dmix = g_out.shape[1]
    assert bp == 1 and nope == HEAD_DIM and vdim == HEAD_DIM and rope <= LANES // 2 * 2
    assert p % LANES == 0 and ts % 16 == 0 and dmix == dsb + dfx + hm * HEAD_DIM
    tp = t + bs * ts
    tm_row = bs * ts
    assert t % tm_row == 0 and tm_row % 16 == 0
    n_assign = tp * TOP_K
    pa = (-(-(n_assign + ne * (MOE_TILE - 1)) // MOE_TILE)) * MOE_TILE
    cfg = dict(T=t, Tp=tp, Bs=bs, Ts=ts, P=p, HS=hs, HF=hf, HM=hm, KVR=kvr, ROPE=rope, NOPE=nope, QR=qr,
               E=ne, DE=de, DSB=dsb, DFX=dfx, DMIX=dmix, prow=bs, PA=pa,
               tm_row=tm_row, tm_mm=_pick_tile(tp, 1056, 16),
               tq_sb=_pick_tile(t, 256, LANES), tq_sm=_pick_tile(t, 512, LANES),
               tc=_pick_tile(tm_row, 128, ts))
    assert bs + 1 <= MODROWS

    c_all = jnp.concatenate([c_sample, c_prompt, jnp.zeros((MODROWS - bs - 1, d), F32)], axis=0)
    mods = _ada(c_all, w_ada, b_ada)
    modf = _ada(c_all, w_final_ada[None], b_final_ada[None])

    c0 = 3 * dsb + 3 * dfx
    zpad = lambda n: jnp.zeros((nl, d, n), F32)
    w_tail = jnp.concatenate([
        w_in[:, :, c0 + hf:c0 + hf + qr + kvr],
        w_in[:, :, c0 + hf + qr + kvr:], zpad(LANES - rope),
        w_in[:, :, c0:c0 + hf], zpad(LANES - hf)], axis=2)
    ntail = w_tail.shape[2]
    w_uq4 = w_uq.reshape(nl, qr, hm, nope + rope)
    w_uq_cat = jnp.concatenate([w_uq4, jnp.zeros((nl, qr, hm, 2 * LANES - nope - rope), F32)],
                               axis=3).reshape(nl, qr, hm * 2 * LANES).astype(BF16)
    w_ukf = w_uk.reshape(nl, kvr, hm * nope).astype(BF16)
    w_uvf = w_uv.reshape(nl, kvr, hm * vdim).astype(BF16)
    cos, sin = _rope_tables(cfg)

    g_attn3, g_mlp3 = g_attn[:, None, :], g_mlp[:, None, :]
    g_q3, g_kv3, b_f3, g_out3 = g_q[:, None, :], g_kv[:, None, :], b_forget[:, None, :], g_out[:, None, :]
    w_router_p = jnp.concatenate([w_router, jnp.zeros((nl, d, LANES - ne), F32)], axis=2)
    b_router3 = jnp.concatenate([b_router, jnp.full((nl, LANES - ne), -jnp.inf, F32)], axis=1)[:, None, :]
    b_gate4, b_up4, b_down4 = b_gate[:, :, None, :], b_up[:, :, None, :], b_down[:, :, None, :]
    kc_sb = cache_sb_k.reshape(nl, bs, p, dsb)
    vc_sb = cache_sb_v.reshape(nl, bs, p, dsb)
    kc_fx = cache_fox_k.reshape(nl, bs, p, dfx)
    vc_fx = cache_fox_v.reshape(nl, bs, p, dfx)
    ckv_c = cache_mla_ckv.reshape(nl, bs * p, kvr)
    kr_c = cache_mla_krope.reshape(nl, bs * p, rope)

    x = jnp.concatenate([x_prompt[0], x_sample.reshape(bs * ts, d)], axis=0)
    tm_mm = cfg["tm_mm"]
    nsb, nfx = dsb // LANES, dfx // LANES
    states = [[] for _ in range(7)]
    for l in range(nl):
        h = _normmod(x, g_attn3, mods, l, 1, 0, cfg, BF16)
        pf32, pbf = _mm(h, w_in, l, c0, (F32, BF16), tm_mm, _pick_tile(c0, 512, LANES))
        (tail,) = _mm(h, w_tail, l, ntail, (F32,), tm_mm, _pick_tile(ntail, 256, LANES))
        cqn, ckv, ckvb, kr, logf = _tailpost(tail, g_q3, g_kv3, b_f3, cos, sin, l, cfg)
        qcat = _qcat(cqn, w_uq_cat, cos, sin, l, cfg)
        kcat, vm = _kcat(ckvb, kr, w_ukf, w_uvf, l, cfg, tm_mm, False)
        kcat_c, vm_c = _kcat(ckv_c, kr_c, w_ukf, w_uvf, l, cfg, _pick_tile(bs * p, 1024, 16), True)

        cum_p = _cumsum(jnp.transpose(logf[:t])[None])[0]
        lf_s = jnp.concatenate([
            jnp.transpose(cache_fox_logf[l], (0, 2, 1)),
            jnp.transpose(logf[t:].reshape(bs, ts, hf), (0, 2, 1)),
            jnp.zeros((bs, hf, LANES - ts), F32)], axis=2)
        cum_s = _cumsum(lf_s)

        o = _sb_prompt(pbf, None, cfg, 0, nsb, 2 * nsb, 0)
        o = _sb_dec(pbf, kc_sb, vc_sb, l, o, cfg, 0, nsb, 2 * nsb, 0)
        f0 = 3 * nsb
        o = _fox_prompt(pbf, cum_p[:, None, :], cum_p[:, :, None], o, cfg, f0, f0 + nfx, f0 + 2 * nfx, nsb)
        o = _fox_dec(pbf, kc_fx, vc_fx, cum_s[:, :, p:p + ts, None], cum_s[:, :, None, :], l, o, cfg,
                     f0, f0 + nfx, f0 + 2 * nfx, nsb)
        o = _mla_prompt(qcat, kcat, vm, o, cfg, nsb + nfx)
        o = _mla_dec(qcat, kcat, vm, kcat_c, vm_c, o, cfg, nsb + nfx)

        x = _gnwo(o, g_out3, w_o, x, mods, l, cfg)
        x = _moe(x, g_mlp3, mods, w_router_p, b_router3, w_gate, b_gate4, w_up, b_up4, w_down, b_down4, l, cfg)

        new_rows = (pf32[:, dsb:2 * dsb], pf32[:, 2 * dsb:3 * dsb],
                    pf32[:, 3 * dsb + dfx:3 * dsb + 2 * dfx], pf32[:, 3 * dsb + 2 * dfx:],
                    logf, ckv, kr)
        for s, nr in zip(states, new_rows):
            s.append(nr)

    y_p, y_s = _normmod(x, g_final[None, None, :], modf, 0, 1, 0, cfg, F32, split_out=True)

    def stack(rows, shape_tail):
        a = jnp.stack(rows)
        return (a[:, :t].reshape((nl, 1, t) + shape_tail), a[:, t:].reshape((nl, bs, ts) + shape_tail))

    tails = [(hs, HEAD_DIM), (hs, HEAD_DIM), (hf, HEAD_DIM), (hf, HEAD_DIM), (hf,), (kvr,), (rope,)]
    st_p, st_s = zip(*[stack(r, tl) for r, tl in zip(states, tails)])
    return (y_p.reshape(1, t, d), y_s.reshape(bs, ts, d)) + tuple(st_p) + tuple(st_s)
```

```python
import functools
import math

import jax
import jax.numpy as jnp
from jax import lax
from jax.experimental import pallas as pl
from jax.experimental.pallas import tpu as pltpu

F32 = jnp.float32
BF16 = jnp.bfloat16
I32 = jnp.int32
U32 = jnp.uint32

HEAD_DIM = 128
LANES = 128
CHUNK = 64
TOP_K = 4
SWIGLU_LIMIT = 7.0
SWIGLU_ALPHA = 1.702
ROPE_THETA = 10000.0
EPS = 1e-6
LOG2E = 1.4426950408889634
NEG_BIG = -1e30
SB_DEAD = -104.0
MOE_TILE = 256
MODROWS = 32
DMA_UNROLL = 8
MIB = 1024 * 1024


def _cparams(sem, vmem_mib):
    return pltpu.CompilerParams(dimension_semantics=sem, vmem_limit_bytes=vmem_mib * MIB)


def _pick_tile(n, target, align):
    best = None
    for t in range(align, min(n, target) + 1, align):
        if n % t == 0:
            best = t
    assert best is not None, (n, target, align)
    return best


def _rms(x):
    return x * lax.rsqrt(jnp.mean(x * x, axis=-1, keepdims=True) + EPS)


def _sigmoid(x):
    return 1.0 / (1.0 + jnp.exp(-x))


def _softplus(z):
    return jnp.maximum(z, 0.0) + jnp.log1p(jnp.exp(-jnp.abs(z)))


def _dot(a, b):
    return jnp.dot(a, b, preferred_element_type=F32)


def _dot_nt(a, b):
    return lax.dot_general(a, b, (((1,), (1,)), ((), ())), preferred_element_type=F32)


def _split2(x):
    hi = x.astype(BF16)
    lo = (x - hi.astype(F32)).astype(BF16)
    return hi, lo


def _ada_kernel(c_ref, w_ref, b_ref, o_ref):
    c = c_ref[...]
    s = (c * _sigmoid(c)).astype(BF16)
    o_ref[...] = _dot(s, w_ref[...].astype(BF16)) + b_ref[...]


def _ada(c_all, w, b):
    nl, d, n = w.shape
    tn = _pick_tile(n, 512, LANES)
    return pl.pallas_call(
        _ada_kernel,
        name="ada",
        grid=(nl, n // tn),
        in_specs=[
            pl.BlockSpec((MODROWS, d), lambda l, j: (0, 0)),
            pl.BlockSpec((None, d, tn), lambda l, j: (l, 0, j)),
            pl.BlockSpec((None, 1, tn), lambda l, j: (l, 0, j)),
        ],
        out_specs=pl.BlockSpec((None, MODROWS, tn), lambda l, j: (l, 0, j)),
        out_shape=jax.ShapeDtypeStruct((nl, MODROWS, n), F32),
        compiler_params=_cparams(("parallel", "parallel"), 40),
    )(c_all, w, b.reshape(nl, 1, n))


def _expand_rows(m, ts):
    nb, d = m.shape
    return jnp.broadcast_to(m[:, None, :], (nb, ts, d)).reshape(nb * ts, d)


def _modulated(i, npt, tm, ts, prow, refs, fn):
    nb = tm // ts

    @pl.when(i < npt)
    def _():
        fn(*[r[prow:prow + 1, :] for r in refs])

    @pl.when(i >= npt)
    def _():
        b0 = pl.multiple_of((i - npt) * nb, nb)
        fn(*[_expand_rows(r[pl.ds(b0, nb), :], ts) for r in refs])


def _normmod_kernel(x_ref, g_ref, sc_ref, sh_ref, *o_refs, npt, tm, ts, prow, split_out):
    i = pl.program_id(0)
    xn = _rms(x_ref[...]) * g_ref[...]

    def emit(sc, sh):
        y = xn * (1.0 + sc) + sh
        if split_out:
            op_ref, os_ref = o_refs

            @pl.when(i < npt)
            def _():
                op_ref[...] = y.astype(op_ref.dtype)

            @pl.when(i >= npt)
            def _():
                os_ref[...] = y.astype(os_ref.dtype)
        else:
            o_refs[0][...] = y.astype(o_refs[0].dtype)

    _modulated(i, npt, tm, ts, prow, (sc_ref, sh_ref), emit)


def _normmod(x, g3, mods, l, k_sc, k_sh, cfg, out_dtype, split_out=False):
    tp, d = x.shape
    tm = cfg["tm_row"]
    npt = cfg["T"] // tm
    nt = tp // tm
    kern = functools.partial(_normmod_kernel, npt=npt, tm=tm, ts=cfg["Ts"], prow=cfg["prow"],
                             split_out=split_out)
    if split_out:
        out_shape = (jax.ShapeDtypeStruct((cfg["T"], d), out_dtype),
                     jax.ShapeDtypeStruct((tp - cfg["T"], d), out_dtype))
        out_specs = (pl.BlockSpec((tm, d), lambda i: (jnp.minimum(i, npt - 1), 0)),
                     pl.BlockSpec((tm, d), lambda i: (jnp.maximum(i - npt, 0), 0)))
    else:
        out_shape = jax.ShapeDtypeStruct((tp, d), out_dtype)
        out_specs = pl.BlockSpec((tm, d), lambda i: (i, 0))
    return pl.pallas_call(
        kern,
        name="normmod",
        grid=(nt,),
        in_specs=[
            pl.BlockSpec((tm, d), lambda i: (i, 0)),
            pl.BlockSpec((None, 1, d), lambda i: (l, 0, 0)),
            pl.BlockSpec((None, MODROWS, d), lambda i: (l, 0, k_sc)),
            pl.BlockSpec((None, MODROWS, d), lambda i: (l, 0, k_sh)),
        ],
        out_specs=out_specs,
        out_shape=out_shape,
        compiler_params=_cparams(("arbitrary",), 40),
    )(x, g3, mods, mods)


def _mm_kernel(x_ref, w_ref, *o_refs, col_scales):
    a = x_ref[...]
    if a.dtype != BF16:
        a = a.astype(BF16)
    acc = _dot(a, w_ref[...].astype(BF16))
    j = pl.program_id(1)
    for o in o_refs:
        if o.dtype == BF16 and col_scales:
            f = jnp.float32(1.0)
            for j0, j1, factor in col_scales:
                f = jnp.where(jnp.logical_and(j >= j0, j < j1), jnp.float32(factor), f)
            o[...] = (acc * f).astype(BF16)
        else:
            o[...] = acc.astype(o.dtype)


def _mm(x, w, l, n, out_dtypes, tm, tn, vmem=48, col_scales=()):
    k = x.shape[-1]
    rows = x.shape[-2]
    x_spec = pl.BlockSpec((tm, k), lambda i, j: (i, 0))
    outs = tuple(jax.ShapeDtypeStruct((rows, n), dt) for dt in out_dtypes)
    res = pl.pallas_call(
        functools.partial(_mm_kernel, col_scales=col_scales),
        name="proj",
        grid=(rows // tm, n // tn),
        in_specs=[x_spec, pl.BlockSpec((None, k, tn), lambda i, j: (l, 0, j))],
        out_specs=tuple(pl.BlockSpec((tm, tn), lambda i, j: (i, j)) for _ in out_dtypes),
        out_shape=outs,
        compiler_params=_cparams(("parallel", "arbitrary"), vmem),
    )(x, w)
    return res


def _rope128(x, cos, sin, half):
    lane = lax.broadcasted_iota(I32, x.shape, 1)
    partner = jnp.where(lane < half, pltpu.roll(x, LANES - half, 1), pltpu.roll(x, half, 1))
    return x * cos + partner * sin


def _tailpost_kernel(t_ref, gq_ref, gkv_ref, bf_ref, cos_ref, sin_ref,
                     cq_ref, ckv_ref, ckvb_ref, kr_ref, lf_ref, *, qr, kvr, rope, hf):
    t = t_ref[...]
    cq_ref[...] = (_rms(t[:, :qr]) * gq_ref[...]).astype(BF16)
    ckv = _rms(t[:, qr:qr + kvr]) * gkv_ref[...]
    ckv_ref[...] = ckv
    ckvb_ref[...] = ckv.astype(BF16)
    kr = _rope128(t[:, qr + kvr:qr + kvr + LANES], cos_ref[...], sin_ref[...], rope // 2)
    kr_ref[...] = kr[:, :rope]
    z = t[:, qr + kvr + LANES:qr + kvr + LANES + hf] + bf_ref[...]
    lf_ref[...] = jnp.minimum(z, 0.0) - jnp.log1p(jnp.exp(-jnp.abs(z)))


def _tailpost(tail, gq3, gkv3, bf3, cos, sin, l, cfg):
    tp, ntail = tail.shape
    tm = cfg["tm_row"]
    qr, kvr, rope, hf = cfg["QR"], cfg["KVR"], cfg["ROPE"], cfg["HF"]
    kern = functools.partial(_tailpost_kernel, qr=qr, kvr=kvr, rope=rope, hf=hf)
    return pl.pallas_call(
        kern,
        name="tailpost",
        grid=(tp // tm,),
        in_specs=[
            pl.BlockSpec((tm, ntail), lambda i: (i, 0)),
            pl.BlockSpec((None, 1, qr), lambda i: (l, 0, 0)),
            pl.BlockSpec((None, 1, kvr), lambda i: (l, 0, 0)),
            pl.BlockSpec((None, 1, hf), lambda i: (l, 0, 0)),
            pl.BlockSpec((tm, LANES), lambda i: (i, 0)),
            pl.BlockSpec((tm, LANES), lambda i: (i, 0)),
        ],
        out_specs=(
            pl.BlockSpec((tm, qr), lambda i: (i, 0)),
            pl.BlockSpec((tm, kvr), lambda i: (i, 0)),
            pl.BlockSpec((tm, kvr), lambda i: (i, 0)),
            pl.BlockSpec((tm, rope), lambda i: (i, 0)),
            pl.BlockSpec((tm, hf), lambda i: (i, 0)),
        ),
        out_shape=(
            jax.ShapeDtypeStruct((tp, qr), BF16),
            jax.ShapeDtypeStruct((tp, kvr), F32),
            jax.ShapeDtypeStruct((tp, kvr), BF16),
            jax.ShapeDtypeStruct((tp, rope), F32),
            jax.ShapeDtypeStruct((tp, hf), F32),
        ),
        compiler_params=_cparams(("parallel",), 32),
    )(tail, gq3, gkv3, bf3, cos, sin)


def _qcat_kernel(x_ref, w_ref, cos_ref, sin_ref, o_ref, *, half, qscale):
    acc = _dot(x_ref[...], w_ref[...]) * qscale
    for g in range(acc.shape[1] // LANES):
        slab = acc[:, g * LANES:(g + 1) * LANES]
        if g % 2 == 1:
            slab = _rope128(slab, cos_ref[...], sin_ref[...], half)
        o_ref[:, g * LANES:(g + 1) * LANES] = slab.astype(o_ref.dtype)


def _qcat(cqn, w_uq_cat, cos, sin, l, cfg):
    tp, qr = cqn.shape
    n = w_uq_cat.shape[-1]
    tm = cfg["tm_mm"]
    tn = _pick_tile(n, 512, 2 * LANES)
    kern = functools.partial(_qcat_kernel, half=cfg["ROPE"] // 2,
                             qscale=LOG2E / math.sqrt(cfg["NOPE"] + cfg["ROPE"]))
    return pl.pallas_call(
        kern,
        name="qcat",
        grid=(tp // tm, n // tn),
        in_specs=[
            pl.BlockSpec((tm, qr), lambda i, j: (i, 0)),
            pl.BlockSpec((None, qr, tn), lambda i, j: (l, 0, j)),
            pl.BlockSpec((tm, LANES), lambda i, j: (i, 0)),
            pl.BlockSpec((tm, LANES), lambda i, j: (i, 0)),
        ],
        out_specs=pl.BlockSpec((tm, tn), lambda i, j: (i, j)),
        out_shape=jax.ShapeDtypeStruct((tp, n), BF16),
        compiler_params=_cparams(("parallel", "arbitrary"), 40),
    )(cqn, w_uq_cat, cos, sin)


def _kcat_kernel(x_ref, kr_ref, wk_ref, wv_ref, k_ref, v_ref, *, rope):
    x = x_ref[...]
    if x.dtype != BF16:
        x = x.astype(BF16)
    kn = _dot(x, wk_ref[...])
    v_ref[...] = _dot(x, wv_ref[...]).astype(v_ref.dtype)
    tm = x.shape[0]
    kr = jnp.concatenate([kr_ref[...], jnp.zeros((tm, LANES - rope), F32)], axis=1).astype(BF16)
    for h in range(kn.shape[1] // LANES):
        k_ref[:, (2 * h) * LANES:(2 * h + 1) * LANES] = kn[:, h * LANES:(h + 1) * LANES].astype(BF16)
        k_ref[:, (2 * h + 1) * LANES:(2 * h + 2) * LANES] = kr


def _kcat(ckv, kr, wk, wv, l, cfg, tm, layered):
    if layered:
        rows = ckv.shape[1]
        x_spec = pl.BlockSpec((None, tm, cfg["KVR"]), lambda i, j: (l, i, 0))
        kr_spec = pl.BlockSpec((None, tm, cfg["ROPE"]), lambda i, j: (l, i, 0))
    else:
        rows = ckv.shape[0]
        x_spec = pl.BlockSpec((tm, cfg["KVR"]), lambda i, j: (i, 0))
        kr_spec = pl.BlockSpec((tm, cfg["ROPE"]), lambda i, j: (i, 0))
    hm = cfg["HM"]
    kern = functools.partial(_kcat_kernel, rope=cfg["ROPE"])
    return pl.pallas_call(
        kern,
        name="kcat",
        grid=(rows // tm, hm // 2),
        in_specs=[
            x_spec, kr_spec,
            pl.BlockSpec((None, cfg["KVR"], 2 * LANES), lambda i, j: (l, 0, j)),
            pl.BlockSpec((None, cfg["KVR"], 2 * LANES), lambda i, j: (l, 0, j)),
        ],
        out_specs=(pl.BlockSpec((tm, 4 * LANES), lambda i, j: (i, j)),
                   pl.BlockSpec((tm, 2 * LANES), lambda i, j: (i, j))),
        out_shape=(jax.ShapeDtypeStruct((rows, hm * 2 * LANES), BF16),
                   jax.ShapeDtypeStruct((rows, hm * LANES), BF16)),
        compiler_params=_cparams(("parallel", "arbitrary"), 40),
    )(ckv, kr, wk, wv)


def _cumsum_kernel(x_ref, o_ref, carry_ref, *, out_scale):
    j = pl.program_id(1)

    @pl.when(j == 0)
    def _():
        carry_ref[...] = jnp.zeros_like(carry_ref)

    x = x_ref[...]
    tb = x.shape[1]
    r = lax.broadcasted_iota(I32, (tb, tb), 0)
    c = lax.broadcasted_iota(I32, (tb, tb), 1)
    upper = jnp.where(r <= c, 1.0, 0.0).astype(BF16)
    x1 = x.astype(BF16)
    r1 = x - x1.astype(F32)
    x2 = r1.astype(BF16)
    x3 = (r1 - x2.astype(F32)).astype(BF16)
    cum = _dot(x1, upper) + _dot(x2, upper) + _dot(x3, upper) + carry_ref[...]
    o_ref[...] = cum * out_scale
    carry_ref[...] = cum[:, tb - 1:tb]


def _cumsum(x, out_scale):
    r, h, tl = x.shape
    tb = _pick_tile(tl, 512, LANES)
    return pl.pallas_call(
        functools.partial(_cumsum_kernel, out_scale=out_scale),
        name="cumsum",
        grid=(r, tl // tb),
        in_specs=[pl.BlockSpec((None, h, tb), lambda i, j: (i, 0, j))],
        out_specs=pl.BlockSpec((None, h, tb), lambda i, j: (i, 0, j)),
        out_shape=jax.ShapeDtypeStruct((r, h, tl), F32),
        scratch_shapes=[pltpu.VMEM((h, 1), F32)],
        compiler_params=_cparams(("parallel", "arbitrary"), 32),
    )(x)


def _sb_chunk(q, k_c, v_c, carry, acc, before, upper):
    tq = q.shape[0]
    z = _dot_nt(q, k_c)
    sp = _softplus(z)
    log1m = -sp if before is None else jnp.where(before, -sp, 0.0)
    hi, lo = _split2(log1m)
    cs = _dot(jnp.concatenate([hi, lo], axis=0), upper)
    suffix = cs[:tq] + cs[tq:] + carry
    w = jnp.exp(z - sp + suffix)
    if before is not None:
        w = jnp.where(before, w, 0.0)
    acc = acc + _dot(w.astype(BF16), v_c)
    carry = carry + jnp.sum(log1m, axis=1, keepdims=True)
    return carry, acc


def _strict_upper():
    r = lax.broadcasted_iota(I32, (LANES, LANES), 0)
    c = lax.broadcasted_iota(I32, (LANES, LANES), 1)
    return jnp.where(r > c, 1.0, 0.0).astype(BF16)


def _sb_prompt_kernel(q_ref, k_ref, v_ref, oin_ref, o_ref, acc_ref, car_ref, *, tq):
    del oin_ref
    qi = pl.program_id(1)
    q = q_ref[...]
    upper = _strict_upper()
    nch = tq // LANES
    acc_ref[...] = jnp.zeros_like(acc_ref)
    car_ref[...] = jnp.zeros_like(car_ref)
    row = lax.broadcasted_iota(I32, (tq, LANES), 0)
    col = lax.broadcasted_iota(I32, (tq, LANES), 1)

    def visit(base, c, masked):
        off = pl.multiple_of(base + c * LANES, LANES)
        before = (col + c * LANES < row) if masked else None
        carry, acc = _sb_chunk(q, k_ref[pl.ds(off, LANES), :], v_ref[pl.ds(off, LANES), :],
                               car_ref[...], acc_ref[...], before, upper)
        car_ref[...] = carry
        acc_ref[...] = acc

    for c in reversed(range(nch)):
        visit(qi * tq, c, True)

    def cond(s):
        kb, dead = s
        return jnp.logical_and(kb >= 0, dead == 0)

    def body(s):
        kb, _ = s
        for c in reversed(range(nch)):
            visit(kb * tq, c, False)
        dead = (jnp.max(car_ref[...]) < SB_DEAD).astype(I32)
        return kb - 1, dead

    lax.while_loop(cond, body, (qi - 1, jnp.int32(0)))
    o_ref[...] = acc_ref[...].astype(o_ref.dtype)


def _pad_rows(x, n):
    return jnp.concatenate([x, jnp.zeros((n - x.shape[0], x.shape[1]), x.dtype)], axis=0)


def _head_cols(h):
    return slice(h * HEAD_DIM, (h + 1) * HEAD_DIM)


def _sb_dec_kernel(q_ref, kn_ref, vn_ref, kc_ref, vc_ref, oin_ref, o_ref, *, ts, p, nh):
    del oin_ref
    upper = _strict_upper()
    row = lax.broadcasted_iota(I32, (ts, LANES), 0)
    col = lax.broadcasted_iota(I32, (ts, LANES), 1)
    for h in range(nh):
        hc = _head_cols(h)
        q = q_ref[:, hc]
        carry = jnp.zeros((ts, 1), F32)
        acc = jnp.zeros((ts, HEAD_DIM), F32)
        carry, acc = _sb_chunk(q, _pad_rows(kn_ref[:, hc], LANES), _pad_rows(vn_ref[:, hc], LANES),
                               carry, acc, col < row, upper)
        for c in reversed(range(p // LANES)):
            k_c = kc_ref[c * LANES:(c + 1) * LANES, h, :].astype(BF16)
            v_c = vc_ref[c * LANES:(c + 1) * LANES, h, :].astype(BF16)
            carry, acc = _sb_chunk(q, k_c, v_c, carry, acc, None, upper)
        o_ref[:, hc] = acc.astype(o_ref.dtype)


PV_CHUNK = 256


def _ones_lane0(rows):
    lane = lax.broadcasted_iota(I32, (rows, LANES), 1)
    return jnp.where(lane == 0, 1.0, 0.0).astype(BF16)


def _flash_block(s, v_of, m_ref, acc_ref):
    tk = s.shape[1]
    m_prev = m_ref[...]
    m_new = jnp.maximum(m_prev, jnp.max(s, axis=1, keepdims=True))
    acc = jnp.exp2(m_prev - m_new) * acc_ref[...]
    ones = _ones_lane0(PV_CHUNK)
    for c in range(tk // PV_CHUNK):
        p = jnp.exp2(s[:, c * PV_CHUNK:(c + 1) * PV_CHUNK] - m_new).astype(BF16)
        acc = acc + _dot(p, jnp.concatenate([v_of(c), ones], axis=1))
    acc_ref[...] = acc
    m_ref[...] = m_new


def _chunk_id(x):
    return lax.shift_right_logical(x, int(math.log2(CHUNK)))


def _sm_prompt_kernel(*refs, tq, fox):
    if fox:
        q_ref, k_ref, v_ref, ck_ref, oin_ref, o_ref, m_ref, acc_ref = refs
    else:
        q_ref, k_ref, v_ref, oin_ref, o_ref, m_ref, acc_ref = refs
    del oin_ref
    qi = pl.program_id(1)
    q = q_ref[...]
    m_ref[...] = jnp.full_like(m_ref, NEG_BIG)
    acc_ref[...] = jnp.zeros_like(acc_ref)

    def block(kb, masked):
        off = pl.multiple_of(kb * tq, tq)

        s = _dot_nt(q, k_ref[pl.ds(off, tq), :])
        if fox:
            s = s - ck_ref[:, pl.ds(off, tq)]
        if masked:
            r = lax.broadcasted_iota(I32, (tq, tq), 0)
            c = lax.broadcasted_iota(I32, (tq, tq), 1)
            valid = (c <= r) if fox else (_chunk_id(c) <= _chunk_id(r))
            s = jnp.where(valid, s, NEG_BIG)

        def v_of(c):
            return v_ref[pl.ds(pl.multiple_of(off + c * PV_CHUNK, PV_CHUNK), PV_CHUNK), :]

        _flash_block(s, v_of, m_ref, acc_ref)

    def body(kb, carry):
        block(kb, False)
        return carry

    lax.fori_loop(0, qi, body, 0)
    block(qi, True)
    acc = acc_ref[...]
    o_ref[...] = (acc[:, :HEAD_DIM] / acc[:, HEAD_DIM:HEAD_DIM + 1]).astype(o_ref.dtype)


def _sm_dec_head(q, kn, vn, kc, vc, ck, ts, p, qpos0):
    s_c = _dot_nt(q, kc)
    s_n = _dot_nt(q, _pad_rows(kn, LANES))
    r = lax.broadcasted_iota(I32, (ts, LANES), 0)
    c = lax.broadcasted_iota(I32, (ts, LANES), 1)
    if ck is not None:
        s_c = s_c - ck[:, :p]
        s_n = s_n - ck[:, p:p + LANES]
        valid = c <= r
    else:
        valid = jnp.logical_and(_chunk_id(c + qpos0) <= _chunk_id(r + qpos0), c < ts)
    s_n = jnp.where(valid, s_n, NEG_BIG)
    m = jnp.maximum(jnp.max(s_c, axis=1, keepdims=True), jnp.max(s_n, axis=1, keepdims=True))
    p_c = jnp.exp2(s_c - m)
    p_n = jnp.exp2(s_n - m)
    den = jnp.sum(p_c, axis=1, keepdims=True) + jnp.sum(p_n, axis=1, keepdims=True)
    num = _dot(p_c.astype(BF16), vc) + _dot(p_n.astype(BF16), _pad_rows(vn, LANES))
    return num / den


def _fox_dec_kernel(q_ref, kn_ref, vn_ref, kc_ref, vc_ref, ck_ref, oin_ref, o_ref, *, ts, p, nh):
    del oin_ref
    for h in range(nh):
        hc = _head_cols(h)
        o = _sm_dec_head(q_ref[:, hc], kn_ref[:, hc], vn_ref[:, hc],
                         kc_ref[:, h, :].astype(BF16), vc_ref[:, h, :].astype(BF16),
                         ck_ref[h], ts, p, p)
        o_ref[:, hc] = o.astype(o_ref.dtype)


def _mla_dec_kernel(q_ref, kn_ref, vn_ref, kc_ref, vc_ref, oin_ref, o_ref, *, ts, p):
    del oin_ref
    o = _sm_dec_head(q_ref[...], kn_ref[...], vn_ref[...], kc_ref[...], vc_ref[...], None, ts, p, p)
    o_ref[...] = o.astype(o_ref.dtype)


def _o_specs(o_prev, tp, dmix):
    shape = jax.ShapeDtypeStruct((tp, dmix), BF16)
    if o_prev is None:
        o_prev = jnp.zeros((8, LANES), BF16)
        return o_prev, shape, {}
    return o_prev, shape, None


def _attn_call(kern, grid, in_specs, args, o_prev, o_spec, tp, dmix, scratch, vmem, n_in):
    o_arg, o_shape, alias = _o_specs(o_prev, tp, dmix)
    aliases = {} if alias is not None else {n_in: 0}
    return pl.pallas_call(
        kern,
        name=kern.func.__name__.strip("_").replace("_kernel", "")
        + {True: "_fox", False: "_mla", None: ""}[kern.keywords.get("fox")],
        grid=grid,
        in_specs=in_specs + [pl.BlockSpec(memory_space=pl.ANY)],
        out_specs=o_spec,
        out_shape=o_shape,
        scratch_shapes=scratch,
        input_output_aliases=aliases,
        compiler_params=_cparams(("parallel", "arbitrary"), vmem),
    )(*args, o_arg)


def _cache_spec(l, p, nh):
    return pl.BlockSpec((None, None, p, nh, LANES), lambda b, _: (l, b, 0, 0, 0))


def _dec_specs(rb, ts, nh, cq0, ck0, cv0, oc0):
    w = nh * LANES
    assert cq0 % nh == 0 and ck0 % nh == 0 and cv0 % nh == 0 and oc0 % nh == 0
    ins = [pl.BlockSpec((ts, w), lambda b, _, c=c0 // nh: (rb + b, c)) for c0 in (cq0, ck0, cv0)]
    return ins, pl.BlockSpec((ts, w), lambda b, _: (rb + b, oc0 // nh))


def _sb_prompt(qkv, o_prev, cfg, cq0, ck0, cv0, oc0):
    t, tp, hs = cfg["T"], cfg["Tp"], cfg["HS"]
    tq = cfg["tq_sb"]
    kern = functools.partial(_sb_prompt_kernel, tq=tq)
    in_specs = [
        pl.BlockSpec((tq, LANES), lambda h, i: (i, cq0 + h)),
        pl.BlockSpec((t, LANES), lambda h, i: (0, ck0 + h)),
        pl.BlockSpec((t, LANES), lambda h, i: (0, cv0 + h)),
    ]
    o_spec = pl.BlockSpec((tq, LANES), lambda h, i: (i, oc0 + h))
    scratch = [pltpu.VMEM((tq, HEAD_DIM), F32), pltpu.VMEM((tq, 1), F32)]
    return _attn_call(kern, (hs, t // tq), in_specs, (qkv, qkv, qkv), o_prev, o_spec, tp, cfg["DMIX"],
                      scratch, 40, 3)


def _sb_dec(qkv, kc, vc, l, o_prev, cfg, cq0, ck0, cv0, oc0):
    t, tp, ts, bs, p, hs = cfg["T"], cfg["Tp"], cfg["Ts"], cfg["Bs"], cfg["P"], cfg["HS"]
    rb = t // ts
    kern = functools.partial(_sb_dec_kernel, ts=ts, p=p, nh=hs)
    in_specs, o_spec = _dec_specs(rb, ts, hs, cq0, ck0, cv0, oc0)
    in_specs += [_cache_spec(l, p, hs), _cache_spec(l, p, hs)]
    return _attn_call(kern, (bs, 1), in_specs, (qkv, qkv, qkv, kc, vc), o_prev, o_spec, tp, cfg["DMIX"],
                      [], 40, 5)


def _sm_scratch(tq):
    return [pltpu.VMEM((tq, 1), F32), pltpu.VMEM((tq, 2 * HEAD_DIM), F32)]


def _fox_prompt(qkv, cum_row, o_prev, cfg, cq0, ck0, cv0, oc0):
    t, tp, hf = cfg["T"], cfg["Tp"], cfg["HF"]
    tq = cfg["tq_sm"]
    kern = functools.partial(_sm_prompt_kernel, tq=tq, fox=True)
    in_specs = [
        pl.BlockSpec((tq, LANES), lambda h, i: (i, cq0 + h)),
        pl.BlockSpec((t, LANES), lambda h, i: (0, ck0 + h)),
        pl.BlockSpec((t, LANES), lambda h, i: (0, cv0 + h)),
        pl.BlockSpec((None, 1, t), lambda h, i: (h, 0, 0)),
    ]
    o_spec = pl.BlockSpec((tq, LANES), lambda h, i: (i, oc0 + h))
    return _attn_call(kern, (hf, t // tq), in_specs, (qkv, qkv, qkv, cum_row), o_prev, o_spec,
                      tp, cfg["DMIX"], _sm_scratch(tq), 48, 4)


def _fox_dec(qkv, kc, vc, ck_s, l, o_prev, cfg, cq0, ck0, cv0, oc0):
    t, tp, ts, bs, p, hf = cfg["T"], cfg["Tp"], cfg["Ts"], cfg["Bs"], cfg["P"], cfg["HF"]
    rb = t // ts
    kern = functools.partial(_fox_dec_kernel, ts=ts, p=p, nh=hf)
    in_specs, o_spec = _dec_specs(rb, ts, hf, cq0, ck0, cv0, oc0)
    in_specs += [_cache_spec(l, p, hf), _cache_spec(l, p, hf),
                 pl.BlockSpec((None, hf, 1, p + LANES), lambda b, _: (b, 0, 0, 0))]
    return _attn_call(kern, (bs, 1), in_specs, (qkv, qkv, qkv, kc, vc, ck_s), o_prev, o_spec, tp,
                      cfg["DMIX"], [], 40, 6)


def _mla_prompt(qcat, kcat, vm, o_prev, cfg, oc0):
    t, tp, hm = cfg["T"], cfg["Tp"], cfg["HM"]
    tq = cfg["tq_sm"]
    kern = functools.partial(_sm_prompt_kernel, tq=tq, fox=False)
    in_specs = [
        pl.BlockSpec((tq, 2 * LANES), lambda h, i: (i, h)),
        pl.BlockSpec((t, 2 * LANES), lambda h, i: (0, h)),
        pl.BlockSpec((t, LANES), lambda h, i: (0, h)),
    ]
    o_spec = pl.BlockSpec((tq, LANES), lambda h, i: (i, oc0 + h))
    return _attn_call(kern, (hm, t // tq), in_specs, (qcat, kcat, vm), o_prev, o_spec, tp, cfg["DMIX"],
                      _sm_scratch(tq), 48, 3)


def _mla_dec(qcat, kcat, vm, kcat_c, vm_c, o_prev, cfg, oc0):
    t, tp, ts, bs, p, hm = cfg["T"], cfg["Tp"], cfg["Ts"], cfg["Bs"], cfg["P"], cfg["HM"]
    rb = t // ts
    kern = functools.partial(_mla_dec_kernel, ts=ts, p=p)
    in_specs = [
        pl.BlockSpec((ts, 2 * LANES), lambda b, h: (rb + b, h)),
        pl.BlockSpec((ts, 2 * LANES), lambda b, h: (rb + b, h)),
        pl.BlockSpec((ts, LANES), lambda b, h: (rb + b, h)),
        pl.BlockSpec((p, 2 * LANES), lambda b, h: (b, h)),
        pl.BlockSpec((p, LANES), lambda b, h: (b, h)),
    ]
    o_spec = pl.BlockSpec((ts, LANES), lambda b, h: (rb + b, oc0 + h))
    return _attn_call(kern, (bs, hm), in_specs, (qcat, kcat, vm, kcat_c, vm_c), o_prev, o_spec, tp,
                      cfg["DMIX"], [], 32, 5)


def _gnwo_kernel(o_ref, g_ref, w_ref, x_ref, gt_ref, yin_ref, y_ref, on_ref, *, bounds, sample, nb, ts, prow):
    del yin_ref
    j = pl.program_id(1)

    @pl.when(j == 0)
    def _():
        for lo, hi in bounds:
            part = o_ref[:, lo:hi].astype(F32)
            on_ref[:, lo:hi] = (_rms(part) * g_ref[:, lo:hi]).astype(BF16)

    mix = _dot(on_ref[...], w_ref[...].astype(BF16))
    gt = _expand_rows(gt_ref[0:nb, :], ts) if sample else gt_ref[prow:prow + 1, :]
    y_ref[...] = x_ref[...] + gt * mix


def _gnwo_part(o, g3, w_o, x, mods, l, cfg, sample, y_prev):
    tp, dmix = o.shape
    d = x.shape[1]
    t, bs, ts = cfg["T"], cfg["Bs"], cfg["Ts"]
    tm = bs * ts if sample else _pick_tile(t, 1024, 16)
    rb0 = t // tm if sample else 0
    rows = bs * ts if sample else t
    tn = _pick_tile(d, 256, LANES)
    bounds = ((0, cfg["DSB"]), (cfg["DSB"], cfg["DSB"] + cfg["DFX"]), (cfg["DSB"] + cfg["DFX"], dmix))
    kern = functools.partial(_gnwo_kernel, bounds=bounds, sample=sample, nb=bs, ts=ts, prow=cfg["prow"])
    if y_prev is None:
        y_prev, aliases = jnp.zeros((8, LANES), F32), {}
    else:
        aliases = {5: 0}
    return pl.pallas_call(
        kern,
        name="gnwo",
        grid=(rows // tm, d // tn),
        in_specs=[
            pl.BlockSpec((tm, dmix), lambda i, j: (i + rb0, 0)),
            pl.BlockSpec((None, 1, dmix), lambda i, j: (l, 0, 0)),
            pl.BlockSpec((None, dmix, tn), lambda i, j: (l, 0, j)),
            pl.BlockSpec((tm, tn), lambda i, j: (i + rb0, j)),
            pl.BlockSpec((None, MODROWS, tn), lambda i, j: (l, 0, 2 * (d // tn) + j)),
            pl.BlockSpec(memory_space=pl.ANY),
        ],
        out_specs=pl.BlockSpec((tm, tn), lambda i, j: (i + rb0, j)),
        out_shape=jax.ShapeDtypeStruct((tp, d), F32),
        scratch_shapes=[pltpu.VMEM((tm, dmix), BF16)],
        input_output_aliases=aliases,
        compiler_params=_cparams(("parallel", "arbitrary"), 48),
    )(o, g3, w_o, x, mods, y_prev)


def _gnwo(o, g3, w_o, x, mods, l, cfg):
    y = _gnwo_part(o, g3, w_o, x, mods, l, cfg, False, None)
    return _gnwo_part(o, g3, w_o, x, mods, l, cfg, True, y)


def _pack_pair(a, b):
    ua = lax.bitcast_convert_type(a.astype(BF16).astype(F32), U32)
    ub = lax.bitcast_convert_type(b.astype(BF16).astype(F32), U32)
    return (ua & jnp.uint32(0xFFFF0000)) | lax.shift_right_logical(ub, jnp.uint32(16))


def _unpack_pair(u):
    hi = lax.bitcast_convert_type(u & jnp.uint32(0xFFFF0000), F32)
    lo = lax.bitcast_convert_type(lax.shift_left(u, jnp.uint32(16)), F32)
    return hi, lo


def _moepre_kernel(x_ref, g_ref, sc_ref, sh_ref, wr_ref, br_ref,
                   hp_ref, idx_ref, gate_ref, pos_ref, cnt_ref, carry_ref, *, npt, tm, ts, prow):
    i = pl.program_id(0)

    @pl.when(i == 0)
    def _():
        carry_ref[...] = jnp.zeros_like(carry_ref)

    xn = _rms(x_ref[...]) * g_ref[...]
    d = xn.shape[1]

    def emit(sc, sh):
        h = xn * (1.0 + sc) + sh
        hp_ref[...] = _pack_pair(h[:, :d // 2], h[:, d // 2:])
        h1, h2 = _split2(h)
        w = wr_ref[...]
        w1, w2 = _split2(w)
        logits = _dot(h1, w1) + (_dot(h1, w2) + _dot(h2, w1)) + br_ref[...]
        lane = lax.broadcasted_iota(I32, (tm, LANES), 1).astype(F32)
        picks, vals, sels = [], [], []
        work = logits
        for _ in range(TOP_K):
            m = jnp.max(work, axis=1, keepdims=True)
            sel = jnp.min(jnp.where(work == m, lane, float(LANES)), axis=1, keepdims=True)
            hit = lane == sel
            picks.append(hit)
            vals.append(m)
            sels.append(sel)
            work = jnp.where(hit, -jnp.inf, work)
        es = [jnp.exp(v - vals[0]) for v in vals]
        den = es[0] + es[1] + es[2] + es[3]
        onehot = jnp.zeros((tm, LANES), F32)
        for hit in picks:
            onehot = onehot + jnp.where(hit, 1.0, 0.0)
        r = lax.broadcasted_iota(I32, (tm, tm), 0)
        c = lax.broadcasted_iota(I32, (tm, tm), 1)
        lower = jnp.where(c < r, 1.0, 0.0).astype(BF16)
        rank = _dot(lower, onehot.astype(BF16)) + carry_ref[...]
        for k in range(TOP_K):
            idx_ref[:, k:k + 1] = sels[k].astype(I32)
            gate_ref[:, k:k + 1] = es[k] / den
            pos_ref[:, k:k + 1] = jnp.sum(jnp.where(picks[k], rank, 0.0), axis=1,
                                          keepdims=True).astype(I32)
        total = carry_ref[...] + jnp.sum(onehot, axis=0, keepdims=True)
        carry_ref[...] = total
        cnt_ref[...] = total.astype(I32)

    _modulated(i, npt, tm, ts, prow, (sc_ref, sh_ref), emit)


def _moepre(x, g3, mods, w_router, b3, l, cfg):
    tp, d = x.shape
    tm = cfg["tm_row"]
    kern = functools.partial(_moepre_kernel, npt=cfg["T"] // tm, tm=tm, ts=cfg["Ts"], prow=cfg["prow"])
    return pl.pallas_call(
        kern,
        name="moepre",
        grid=(tp // tm,),
        in_specs=[
            pl.BlockSpec((tm, d), lambda i: (i, 0)),
            pl.BlockSpec((None, 1, d), lambda i: (l, 0, 0)),
            pl.BlockSpec((None, MODROWS, d), lambda i: (l, 0, 4)),
            pl.BlockSpec((None, MODROWS, d), lambda i: (l, 0, 3)),
            pl.BlockSpec((None, d, LANES), lambda i: (l, 0, 0)),
            pl.BlockSpec((None, 1, LANES), lambda i: (l, 0, 0)),
        ],
        out_specs=(
            pl.BlockSpec((tm, d // 2), lambda i: (i, 0)),
            pl.BlockSpec((tm, TOP_K), lambda i: (i, 0)),
            pl.BlockSpec((tm, TOP_K), lambda i: (i, 0)),
            pl.BlockSpec((tm, TOP_K), lambda i: (i, 0)),
            pl.BlockSpec((1, LANES), lambda i: (0, 0)),
        ),
        out_shape=(
            jax.ShapeDtypeStruct((tp, d // 2), U32),
            jax.ShapeDtypeStruct((tp, TOP_K), I32),
            jax.ShapeDtypeStruct((tp, TOP_K), F32),
            jax.ShapeDtypeStruct((tp, TOP_K), I32),
            jax.ShapeDtypeStruct((1, LANES), I32),
        ),
        scratch_shapes=[pltpu.VMEM((1, LANES), F32)],
        compiler_params=_cparams(("arbitrary",), 40),
    )(x, g3, mods, mods, w_router, b3)


def _slot_token(a):
    return lax.shift_right_logical(a, int(math.log2(TOP_K)))


def _slot_choice(a):
    return lax.bitwise_and(a, TOP_K - 1)


def _dispatch_kernel(zrow_ref, dest_ref, hp_ref, xs_ref, zero_ref, zsem, sem, *, td, ne):
    i = pl.program_id(0)

    def zero_copy(e):
        r0 = pl.multiple_of(zrow_ref[e], MOE_TILE)
        return pltpu.make_async_copy(zero_ref, xs_ref.at[pl.ds(r0, MOE_TILE)], zsem)

    @pl.when(i == 0)
    def _():
        zero_ref[...] = jnp.zeros_like(zero_ref)
        for e in range(ne):
            zero_copy(e).start()
        for e in range(ne):
            zero_copy(e).wait()

    n = td * TOP_K

    def row_copy(a):
        return pltpu.make_async_copy(hp_ref.at[pl.ds(_slot_token(a), 1)],
                                     xs_ref.at[pl.ds(dest_ref[0, 0, a], 1)], sem)

    def start(a, c):
        row_copy(a).start()
        return c

    def wait(a, c):
        row_copy(a).wait()
        return c

    lax.fori_loop(0, n, start, 0, unroll=DMA_UNROLL)
    lax.fori_loop(0, n, wait, 0, unroll=DMA_UNROLL)


def _dispatch(hp, dest, zrow, cfg):
    tp, dh = hp.shape
    td = cfg["tm_row"]
    nt = tp // td
    kern = functools.partial(_dispatch_kernel, td=td, ne=cfg["E"])
    grid_spec = pltpu.PrefetchScalarGridSpec(
        num_scalar_prefetch=1,
        grid=(nt,),
        in_specs=[
            pl.BlockSpec((1, 1, td * TOP_K), lambda i, z: (i, 0, 0), memory_space=pltpu.SMEM),
            pl.BlockSpec((td, dh), lambda i, z: (i, 0)),
        ],
        out_specs=pl.BlockSpec(memory_space=pl.ANY),
        scratch_shapes=[pltpu.VMEM((MOE_TILE, dh), U32), pltpu.SemaphoreType.DMA(()),
                        pltpu.SemaphoreType.DMA(())],
    )
    return pl.pallas_call(
        kern,
        name="dispatch",
        grid_spec=grid_spec,
        out_shape=jax.ShapeDtypeStruct((cfg["PA"], dh), U32),
        compiler_params=pltpu.CompilerParams(dimension_semantics=("arbitrary",), vmem_limit_bytes=32 * MIB,
                                             has_side_effects=True),
    )(zrow, dest.reshape(nt, 1, td * TOP_K), hp)


def _tile_is_first(te_ref, i):
    prev = te_ref[jnp.maximum(i - 1, 0)]
    return jnp.logical_or(i == 0, te_ref[i] != prev)


def _g1_kernel(te_ref, nv_ref, xs_ref, wg_ref, bg_ref, wu_ref, bu_ref, act_ref, wgb_ref, wub_ref):
    i = pl.program_id(1)

    @pl.when(i < nv_ref[0])
    def _():
        @pl.when(_tile_is_first(te_ref, i))
        def _():
            wgb_ref[...] = wg_ref[...].astype(BF16)
            wub_ref[...] = wu_ref[...].astype(BF16)

        xa, xb = _unpack_pair(xs_ref[...])
        xa = xa.astype(BF16)
        xb = xb.astype(BF16)
        dh = xa.shape[1]
        g = _dot(xa, wgb_ref[:dh, :]) + _dot(xb, wgb_ref[dh:, :]) + bg_ref[...]
        u = _dot(xa, wub_ref[:dh, :]) + _dot(xb, wub_ref[dh:, :]) + bu_ref[...]
        g = jnp.minimum(g, SWIGLU_LIMIT)
        u = jnp.clip(u, -SWIGLU_LIMIT, SWIGLU_LIMIT)
        act_ref[...] = ((u + 1.0) * g * _sigmoid(SWIGLU_ALPHA * g)).astype(act_ref.dtype)


def _g1(xs, w_gate, b_gate4, w_up, b_up4, te, nv, l, cfg):
    pa, dh = xs.shape
    d, de = w_gate.shape[2], w_gate.shape[3]
    tn = _pick_tile(de, 512, LANES)
    nt = pa // MOE_TILE

    def row(i, te_ref, nv_ref):
        return jnp.minimum(i, nv_ref[0] - 1)

    grid_spec = pltpu.PrefetchScalarGridSpec(
        num_scalar_prefetch=2,
        grid=(de // tn, nt),
        in_specs=[
            pl.BlockSpec((MOE_TILE, dh), lambda j, i, te_ref, nv_ref: (row(i, te_ref, nv_ref), 0)),
            pl.BlockSpec((None, None, d, tn), lambda j, i, te_ref, nv_ref: (l, te_ref[i], 0, j)),
            pl.BlockSpec((None, None, 1, tn), lambda j, i, te_ref, nv_ref: (l, te_ref[i], 0, j)),
            pl.BlockSpec((None, None, d, tn), lambda j, i, te_ref, nv_ref: (l, te_ref[i], 0, j)),
            pl.BlockSpec((None, None, 1, tn), lambda j, i, te_ref, nv_ref: (l, te_ref[i], 0, j)),
        ],
        out_specs=pl.BlockSpec((MOE_TILE, tn), lambda j, i, te_ref, nv_ref: (row(i, te_ref, nv_ref), j)),
        scratch_shapes=[pltpu.VMEM((d, tn), BF16), pltpu.VMEM((d, tn), BF16)],
    )
    return pl.pallas_call(
        _g1_kernel,
        name="expert_up",
        grid_spec=grid_spec,
        out_shape=jax.ShapeDtypeStruct((pa, de), BF16),
        compiler_params=_cparams(("arbitrary", "arbitrary"), 56),
    )(te, nv, xs, w_gate, b_gate4, w_up, b_up4)


def _g2_kernel(te_ref, nv_ref, act_ref, wd_ref, bd_ref, ys_ref, wdb_ref):
    i = pl.program_id(1)

    @pl.when(i < nv_ref[0])
    def _():
        @pl.when(_tile_is_first(te_ref, i))
        def _():
            wdb_ref[...] = wd_ref[...].astype(BF16)

        y = _dot(act_ref[...], wdb_ref[...]) + bd_ref[...]
        half = y.shape[1] // 2
        ys_ref[...] = _pack_pair(y[:, :half], y[:, half:])


def _g2(act, w_down, b_down4, te, nv, l, cfg):
    pa, de = act.shape
    d = w_down.shape[3]
    tn = _pick_tile(d, 2048, 2 * LANES)
    nt = pa // MOE_TILE

    def row(i, te_ref, nv_ref):
        return jnp.minimum(i, nv_ref[0] - 1)

    grid_spec = pltpu.PrefetchScalarGridSpec(
        num_scalar_prefetch=2,
        grid=(d // tn, nt),
        in_specs=[
            pl.BlockSpec((MOE_TILE, de), lambda j, i, te_ref, nv_ref: (row(i, te_ref, nv_ref), 0)),
            pl.BlockSpec((None, None, de, tn), lambda j, i, te_ref, nv_ref: (l, te_ref[i], 0, j)),
            pl.BlockSpec((None, None, 1, tn), lambda j, i, te_ref, nv_ref: (l, te_ref[i], 0, j)),
        ],
        out_specs=pl.BlockSpec((MOE_TILE, tn // 2), lambda j, i, te_ref, nv_ref: (row(i, te_ref, nv_ref), j)),
        scratch_shapes=[pltpu.VMEM((de, tn), BF16)],
    )
    return pl.pallas_call(
        _g2_kernel,
        name="expert_down",
        grid_spec=grid_spec,
        out_shape=jax.ShapeDtypeStruct((pa, d // 2), U32),
        compiler_params=_cparams(("arbitrary", "arbitrary"), 48),
    )(te, nv, act, w_down, b_down4), tn


def _combine_kernel(dcur_ref, dnxt_ref, ys_ref, gate_ref, x_ref, gt_ref, o_ref, buf_ref, sem,
                    *, tc, nt, tn, npt, ts, prow):
    i = pl.program_id(0)
    slot = lax.rem(i, 2)
    n = tc * TOP_K

    def row_copy(d_ref, s, a):
        return pltpu.make_async_copy(ys_ref.at[pl.ds(d_ref[0, 0, a], 1)],
                                     buf_ref.at[s, _slot_choice(a), pl.ds(_slot_token(a), 1)], sem.at[s])

    def issue(d_ref, s):
        lax.fori_loop(0, n, lambda a, c: (row_copy(d_ref, s, a).start(), c)[1], 0, unroll=DMA_UNROLL)

    @pl.when(i == 0)
    def _():
        issue(dcur_ref, 0)

    @pl.when(i + 1 < nt)
    def _():
        issue(dnxt_ref, 1 - slot)

    lax.fori_loop(0, n, lambda a, c: (row_copy(dcur_ref, slot, a).wait(), c)[1], 0, unroll=DMA_UNROLL)

    gate = gate_ref[...]
    half = tn // 2
    dh = buf_ref.shape[-1]
    parts = [None] * (2 * (dh // half))
    for k in range(TOP_K):
        hi, lo = _unpack_pair(buf_ref[slot, k])
        gk = gate[:, k:k + 1]
        for j in range(dh // half):
            a = gk * hi[:, j * half:(j + 1) * half]
            b = gk * lo[:, j * half:(j + 1) * half]
            parts[2 * j] = a if parts[2 * j] is None else parts[2 * j] + a
            parts[2 * j + 1] = b if parts[2 * j + 1] is None else parts[2 * j + 1] + b
    moe = jnp.concatenate(parts, axis=1)

    def emit(gt):
        o_ref[...] = x_ref[...] + gt * moe

    _modulated(i, npt, tc, ts, prow, (gt_ref,), emit)


def _combine(ys, dest, gate, x, mods, l, tn, cfg):
    tp, d = x.shape
    tc = cfg["tc"]
    nt = tp // tc
    dh = ys.shape[1]
    kern = functools.partial(_combine_kernel, tc=tc, nt=nt, tn=tn, npt=cfg["T"] // tc, ts=cfg["Ts"],
                             prow=cfg["prow"])
    dest3 = dest.reshape(nt, 1, tc * TOP_K)
    return pl.pallas_call(
        kern,
        name="combine",
        grid=(nt,),
        in_specs=[
            pl.BlockSpec((1, 1, tc * TOP_K), lambda i: (i, 0, 0), memory_space=pltpu.SMEM),
            pl.BlockSpec((1, 1, tc * TOP_K), lambda i: (jnp.minimum(i + 1, nt - 1), 0, 0),
                         memory_space=pltpu.SMEM),
            pl.BlockSpec(memory_space=pl.ANY),
            pl.BlockSpec((tc, TOP_K), lambda i: (i, 0)),
            pl.BlockSpec((tc, d), lambda i: (i, 0)),
            pl.BlockSpec((None, MODROWS, d), lambda i: (l, 0, 5)),
        ],
        out_specs=pl.BlockSpec((tc, d), lambda i: (i, 0)),
        out_shape=jax.ShapeDtypeStruct((tp, d), F32),
        scratch_shapes=[pltpu.VMEM((2, TOP_K, tc, dh), U32), pltpu.SemaphoreType.DMA((2,))],
        compiler_params=_cparams(("arbitrary",), 40),
    )(dest3, dest3, ys, gate, x, mods)


def _moe(x, g3, mods, w_router, b_router3, w_gate, b_gate4, w_up, b_up4, w_down, b_down4, l, cfg):
    ne = cfg["E"]
    hp, idx, gate, pos, cnt = _moepre(x, g3, mods, w_router, b_router3, l, cfg)
    cnt = cnt[0, :ne]
    pc = ((cnt + MOE_TILE - 1) // MOE_TILE) * MOE_TILE
    ends = jnp.cumsum(pc)
    off = ends - pc
    dest = off[idx] + pos
    nt = cfg["PA"] // MOE_TILE
    tile_ends = ends // MOE_TILE
    nv = tile_ends[-1]
    tiles = jnp.minimum(jnp.arange(nt, dtype=I32), nv - 1)
    te = jnp.minimum(jnp.sum((tile_ends[None, :] <= tiles[:, None]).astype(I32), axis=1), ne - 1)
    zrow = jnp.clip(ends - MOE_TILE, 0, cfg["PA"] - MOE_TILE).astype(I32)
    nv1 = nv.reshape(1).astype(I32)
    xs = _dispatch(hp, dest.astype(I32), zrow, cfg)
    act = _g1(xs, w_gate, b_gate4, w_up, b_up4, te, nv1, l, cfg)
    ys, tn = _g2(act, w_down, b_down4, te, nv1, l, cfg)
    return _combine(ys, dest.astype(I32), gate, x, mods, l, tn, cfg)


def _rope_tables(cfg):
    half = cfg["ROPE"] // 2
    inv_freq = ROPE_THETA ** (-jnp.arange(half, dtype=F32) / half)
    pos = jnp.concatenate([jnp.arange(cfg["T"], dtype=I32),
                           jnp.tile(cfg["P"] + jnp.arange(cfg["Ts"], dtype=I32), cfg["Bs"])])
    ang = pos.astype(F32)[:, None] * inv_freq[None, :]
    cos, sin = jnp.cos(ang), jnp.sin(ang)
    pad = jnp.zeros((pos.shape[0], LANES - 2 * half), F32)
    return (jnp.concatenate([cos, cos, pad], axis=1), jnp.concatenate([-sin, sin, pad], axis=1))


def kernel(x_prompt, x_sample, c_prompt, c_sample, cache_sb_k, cache_sb_v, cache_fox_k, cache_fox_v,
           cache_fox_logf, cache_mla_ckv, cache_mla_krope, w_in, b_forget, g_q, w_uq, g_kv, w_uk, w_uv,
           g_out, w_o, g_attn, g_mlp, w_ada, b_ada, w_router, b_router, w_gate, b_gate, w_up, b_up,
           w_down, b_down, g_final, w_final_ada, b_final_ada):
    nl, d, _ = w_in.shape
    bp, t, _ = x_prompt.shape
    bs, ts, _ = x_sample.shape
    p = cache_sb_k.shape[2]
    hs, hf = cache_sb_k.shape[3], cache_fox_k.shape[3]
    kvr, rope = cache_mla_ckv.shape[-1], cache_mla_krope.shape[-1]
    hm, nope, vdim = w_uk.shape[2], w_uk.shape[3], w_uv.shape[3]
    qr = g_q.shape[1]
    ne, de = w_router.shape[-1], w_gate.shape[-1]
    dsb, dfx = hs * HEAD_DIM, hf * HEAD_DIM
    dmix = g_out.shape[1]
    assert bp == 1 and nope == HEAD_DIM and vdim == HEAD_DIM and rope <= LANES // 2 * 2
    assert p % LANES == 0 and ts % 16 == 0 and dmix == dsb + dfx + hm * HEAD_DIM
    tp = t + bs * ts
    tm_row = bs * ts
    assert t % tm_row == 0 and tm_row % 16 == 0
    n_assign = tp * TOP_K
    pa = (-(-(n_assign + ne * (MOE_TILE - 1)) // MOE_TILE)) * MOE_TILE
    cfg = dict(T=t, Tp=tp, Bs=bs, Ts=ts, P=p, HS=hs, HF=hf, HM=hm, KVR=kvr, ROPE=rope, NOPE=nope, QR=qr,
               E=ne, DE=de, DSB=dsb, DFX=dfx, DMIX=dmix, prow=bs, PA=pa,
               tm_row=tm_row, tm_mm=_pick_tile(tp, 1056, 16),
               tq_sb=_pick_tile(t, 256, LANES), tq_sm=_pick_tile(t, 1024, PV_CHUNK),
               tc=_pick_tile(tm_row, 128, ts))
    assert bs + 1 <= MODROWS

    c_all = jnp.concatenate([c_sample, c_prompt, jnp.zeros((MODROWS - bs - 1, d), F32)], axis=0)
    mods = _ada(c_all, w_ada, b_ada)
    modf = _ada(c_all, w_final_ada[None], b_final_ada[None])

    c0 = 3 * dsb + 3 * dfx
    zpad = lambda n: jnp.zeros((nl, d, n), F32)
    w_tail = jnp.concatenate([
        w_in[:, :, c0 + hf:c0 + hf + qr + kvr],
        w_in[:, :, c0 + hf + qr + kvr:], zpad(LANES - rope),
        w_in[:, :, c0:c0 + hf], zpad(LANES - hf)], axis=2)
    ntail = w_tail.shape[2]
    w_uq4 = w_uq.reshape(nl, qr, hm, nope + rope)
    w_uq_cat = jnp.concatenate([w_uq4, jnp.zeros((nl, qr, hm, 2 * LANES - nope - rope), F32)],
                               axis=3).reshape(nl, qr, hm * 2 * LANES).astype(BF16)
    w_ukf = w_uk.reshape(nl, kvr, hm * nope).astype(BF16)
    w_uvf = w_uv.reshape(nl, kvr, hm * vdim).astype(BF16)
    cos, sin = _rope_tables(cfg)

    g_attn3, g_mlp3 = g_attn[:, None, :], g_mlp[:, None, :]
    g_q3, g_kv3, b_f3, g_out3 = g_q[:, None, :], g_kv[:, None, :], b_forget[:, None, :], g_out[:, None, :]
    w_router_p = jnp.concatenate([w_router, jnp.zeros((nl, d, LANES - ne), F32)], axis=2)
    b_router3 = jnp.concatenate([b_router, jnp.full((nl, LANES - ne), -jnp.inf, F32)], axis=1)[:, None, :]
    b_gate4, b_up4, b_down4 = b_gate[:, :, None, :], b_up[:, :, None, :], b_down[:, :, None, :]
    ckv_c = cache_mla_ckv.reshape(nl, bs * p, kvr)
    kr_c = cache_mla_krope.reshape(nl, bs * p, rope)

    x = jnp.concatenate([x_prompt[0], x_sample.reshape(bs * ts, d)], axis=0)
    tm_mm = cfg["tm_mm"]
    nsb, nfx = dsb // LANES, dfx // LANES
    states = [[] for _ in range(7)]
    for l in range(nl):
        h = _normmod(x, g_attn3, mods, l, 1, 0, cfg, BF16)
        tn_in = _pick_tile(math.gcd(dsb, dfx), 512, LANES)
        qs = 1.0 / math.sqrt(HEAD_DIM)
        col_scales = ((0, dsb // tn_in, qs), (3 * dsb // tn_in, (3 * dsb + dfx) // tn_in, qs * LOG2E))
        pf32, pbf = _mm(h, w_in, l, c0, (F32, BF16), tm_mm, tn_in, col_scales=col_scales)
        (tail,) = _mm(h, w_tail, l, ntail, (F32,), tm_mm, _pick_tile(ntail, 256, LANES))
        cqn, ckv, ckvb, kr, logf = _tailpost(tail, g_q3, g_kv3, b_f3, cos, sin, l, cfg)
        qcat = _qcat(cqn, w_uq_cat, cos, sin, l, cfg)
        kcat, vm = _kcat(ckvb, kr, w_ukf, w_uvf, l, cfg, tm_mm, False)
        kcat_c, vm_c = _kcat(ckv_c, kr_c, w_ukf, w_uvf, l, cfg, _pick_tile(bs * p, 1024, 16), True)

        cum_p = _cumsum(jnp.transpose(logf[:t])[None], LOG2E)[0]
        lf_s = jnp.concatenate([
            jnp.transpose(cache_fox_logf[l], (0, 2, 1)),
            jnp.transpose(logf[t:].reshape(bs, ts, hf), (0, 2, 1)),
            jnp.zeros((bs, hf, LANES - ts), F32)], axis=2)
        cum_s = _cumsum(lf_s, LOG2E)

        o = _sb_prompt(pbf, None, cfg, 0, nsb, 2 * nsb, 0)
        o = _sb_dec(pbf, cache_sb_k, cache_sb_v, l, o, cfg, 0, nsb, 2 * nsb, 0)
        f0 = 3 * nsb
        o = _fox_prompt(pbf, cum_p[:, None, :], o, cfg, f0, f0 + nfx, f0 + 2 * nfx, nsb)
        o = _fox_dec(pbf, cache_fox_k, cache_fox_v, cum_s[:, :, None, :], l, o, cfg,
                     f0, f0 + nfx, f0 + 2 * nfx, nsb)
        o = _mla_prompt(qcat, kcat, vm, o, cfg, nsb + nfx)
        o = _mla_dec(qcat, kcat, vm, kcat_c, vm_c, o, cfg, nsb + nfx)

        x = _gnwo(o, g_out3, w_o, x, mods, l, cfg)
        x = _moe(x, g_mlp3, mods, w_router_p, b_router3, w_gate, b_gate4, w_up, b_up4, w_down, b_down4, l, cfg)

        new_rows = (pf32[:, dsb:2 * dsb], pf32[:, 2 * dsb:3 * dsb],
                    pf32[:, 3 * dsb + dfx:3 * dsb + 2 * dfx], pf32[:, 3 * dsb + 2 * dfx:],
                    logf, ckv, kr)
        for s, nr in zip(states, new_rows):
            s.append(nr)

    y_p, y_s = _normmod(x, g_final[None, None, :], modf, 0, 1, 0, cfg, F32, split_out=True)

    def stack(rows, shape_tail):
        return (jnp.stack([a[:t].reshape((1, t) + shape_tail) for a in rows]),
                jnp.stack([a[t:].reshape((bs, ts) + shape_tail) for a in rows]))

    tails = [(hs, HEAD_DIM), (hs, HEAD_DIM), (hf, HEAD_DIM), (hf, HEAD_DIM), (hf,), (kvr,), (rope,)]
    st_p, st_s = zip(*[stack(r, tl) for r, tl in zip(states, tails)])
    return (y_p.reshape(1, t, d), y_s.reshape(bs, ts, d)) + tuple(st_p) + tuple(st_s)
```

```python
import functools
import math

import jax
import jax.numpy as jnp
from jax import lax
from jax.experimental import pallas as pl
from jax.experimental.pallas import tpu as pltpu

F32 = jnp.float32
BF16 = jnp.bfloat16
I32 = jnp.int32
U32 = jnp.uint32

HEAD_DIM = 128
LANES = 128
CHUNK = 64
TOP_K = 4
SWIGLU_LIMIT = 7.0
SWIGLU_ALPHA = 1.702
ROPE_THETA = 10000.0
EPS = 1e-6
LOG2E = 1.4426950408889634
NEG_BIG = -1e30
SB_DEAD = -104.0
MOE_TILE = 256
MODROWS = 32
DMA_UNROLL = 8
MIB = 1024 * 1024


def _cparams(sem, vmem_mib):
    return pltpu.CompilerParams(dimension_semantics=sem, vmem_limit_bytes=vmem_mib * MIB)


def _pick_tile(n, target, align):
    best = None
    for t in range(align, min(n, target) + 1, align):
        if n % t == 0:
            best = t
    assert best is not None, (n, target, align)
    return best


def _rms(x):
    return x * lax.rsqrt(jnp.mean(x * x, axis=-1, keepdims=True) + EPS)


def _sigmoid(x):
    return 1.0 / (1.0 + jnp.exp(-x))


def _softplus(z):
    return jnp.maximum(z, 0.0) + jnp.log1p(jnp.exp(-jnp.abs(z)))


def _dot(a, b):
    return jnp.dot(a, b, preferred_element_type=F32)


def _dot_nt(a, b):
    return lax.dot_general(a, b, (((1,), (1,)), ((), ())), preferred_element_type=F32)


def _split2(x):
    hi = x.astype(BF16)
    lo = (x - hi.astype(F32)).astype(BF16)
    return hi, lo


def _ada_kernel(c_ref, w_ref, b_ref, o_ref):
    c = c_ref[...]
    s = (c * _sigmoid(c)).astype(BF16)
    o_ref[...] = _dot(s, w_ref[...].astype(BF16)) + b_ref[...]


def _ada(c_all, w, b):
    nl, d, n = w.shape
    tn = _pick_tile(n, 512, LANES)
    return pl.pallas_call(
        _ada_kernel,
        name="ada",
        grid=(nl, n // tn),
        in_specs=[
            pl.BlockSpec((MODROWS, d), lambda l, j: (0, 0)),
            pl.BlockSpec((None, d, tn), lambda l, j: (l, 0, j)),
            pl.BlockSpec((None, 1, tn), lambda l, j: (l, 0, j)),
        ],
        out_specs=pl.BlockSpec((None, MODROWS, tn), lambda l, j: (l, 0, j)),
        out_shape=jax.ShapeDtypeStruct((nl, MODROWS, n), F32),
        compiler_params=_cparams(("parallel", "parallel"), 40),
    )(c_all, w, b.reshape(nl, 1, n))


def _expand_rows(m, ts):
    nb, d = m.shape
    return jnp.broadcast_to(m[:, None, :], (nb, ts, d)).reshape(nb * ts, d)


def _modulated(i, npt, tm, ts, prow, refs, fn):
    nb = tm // ts

    @pl.when(i < npt)
    def _():
        fn(*[r[prow:prow + 1, :] for r in refs])

    @pl.when(i >= npt)
    def _():
        b0 = pl.multiple_of((i - npt) * nb, nb)
        fn(*[_expand_rows(r[pl.ds(b0, nb), :], ts) for r in refs])


def _normmod_kernel(x_ref, g_ref, sc_ref, sh_ref, *o_refs, npt, tm, ts, prow, split_out):
    i = pl.program_id(0)
    xn = _rms(x_ref[...]) * g_ref[...]

    def emit(sc, sh):
        y = xn * (1.0 + sc) + sh
        if split_out:
            op_ref, os_ref = o_refs

            @pl.when(i < npt)
            def _():
                op_ref[...] = y.astype(op_ref.dtype)

            @pl.when(i >= npt)
            def _():
                os_ref[...] = y.astype(os_ref.dtype)
        else:
            o_refs[0][...] = y.astype(o_refs[0].dtype)

    _modulated(i, npt, tm, ts, prow, (sc_ref, sh_ref), emit)


def _normmod(x, g3, mods, l, k_sc, k_sh, cfg, out_dtype, split_out=False):
    tp, d = x.shape
    tm = cfg["tm_row"]
    npt = cfg["T"] // tm
    nt = tp // tm
    kern = functools.partial(_normmod_kernel, npt=npt, tm=tm, ts=cfg["Ts"], prow=cfg["prow"],
                             split_out=split_out)
    if split_out:
        out_shape = (jax.ShapeDtypeStruct((cfg["T"], d), out_dtype),
                     jax.ShapeDtypeStruct((tp - cfg["T"], d), out_dtype))
        out_specs = (pl.BlockSpec((tm, d), lambda i: (jnp.minimum(i, npt - 1), 0)),
                     pl.BlockSpec((tm, d), lambda i: (jnp.maximum(i - npt, 0), 0)))
    else:
        out_shape = jax.ShapeDtypeStruct((tp, d), out_dtype)
        out_specs = pl.BlockSpec((tm, d), lambda i: (i, 0))
    return pl.pallas_call(
        kern,
        name="normmod",
        grid=(nt,),
        in_specs=[
            pl.BlockSpec((tm, d), lambda i: (i, 0)),
            pl.BlockSpec((None, 1, d), lambda i: (l, 0, 0)),
            pl.BlockSpec((None, MODROWS, d), lambda i: (l, 0, k_sc)),
            pl.BlockSpec((None, MODROWS, d), lambda i: (l, 0, k_sh)),
        ],
        out_specs=out_specs,
        out_shape=out_shape,
        compiler_params=_cparams(("arbitrary",), 40),
    )(x, g3, mods, mods)


def _mm_kernel(x_ref, w_ref, *o_refs, col_scales):
    a = x_ref[...]
    if a.dtype != BF16:
        a = a.astype(BF16)
    acc = _dot(a, w_ref[...].astype(BF16))
    j = pl.program_id(1)
    for o in o_refs:
        if o.dtype == BF16 and col_scales:
            f = jnp.float32(1.0)
            for j0, j1, factor in col_scales:
                f = jnp.where(jnp.logical_and(j >= j0, j < j1), jnp.float32(factor), f)
            o[...] = (acc * f).astype(BF16)
        else:
            o[...] = acc.astype(o.dtype)


def _mm(x, w, l, n, out_dtypes, tm, tn, vmem=48, col_scales=()):
    k = x.shape[-1]
    rows = x.shape[-2]
    x_spec = pl.BlockSpec((tm, k), lambda i, j: (i, 0))
    outs = tuple(jax.ShapeDtypeStruct((rows, n), dt) for dt in out_dtypes)
    res = pl.pallas_call(
        functools.partial(_mm_kernel, col_scales=col_scales),
        name="proj",
        grid=(rows // tm, n // tn),
        in_specs=[x_spec, pl.BlockSpec((None, k, tn), lambda i, j: (l, 0, j))],
        out_specs=tuple(pl.BlockSpec((tm, tn), lambda i, j: (i, j)) for _ in out_dtypes),
        out_shape=outs,
        compiler_params=_cparams(("parallel", "arbitrary"), vmem),
    )(x, w)
    return res


def _rope128(x, cos, sin, half):
    lane = lax.broadcasted_iota(I32, x.shape, 1)
    partner = jnp.where(lane < half, pltpu.roll(x, LANES - half, 1), pltpu.roll(x, half, 1))
    return x * cos + partner * sin


def _tailpost_kernel(t_ref, gq_ref, gkv_ref, bf_ref, cos_ref, sin_ref,
                     cq_ref, ckv_ref, ckvb_ref, kr_ref, lf_ref, *, qr, kvr, rope, hf):
    t = t_ref[...]
    cq_ref[...] = (_rms(t[:, :qr]) * gq_ref[...]).astype(BF16)
    ckv = _rms(t[:, qr:qr + kvr]) * gkv_ref[...]
    ckv_ref[...] = ckv
    ckvb_ref[...] = ckv.astype(BF16)
    kr = _rope128(t[:, qr + kvr:qr + kvr + LANES], cos_ref[...], sin_ref[...], rope // 2)
    kr_ref[...] = kr[:, :rope]
    z = t[:, qr + kvr + LANES:qr + kvr + LANES + hf] + bf_ref[...]
    lf_ref[...] = jnp.minimum(z, 0.0) - jnp.log1p(jnp.exp(-jnp.abs(z)))


def _tailpost(tail, gq3, gkv3, bf3, cos, sin, l, cfg):
    tp, ntail = tail.shape
    tm = cfg["tm_row"]
    qr, kvr, rope, hf = cfg["QR"], cfg["KVR"], cfg["ROPE"], cfg["HF"]
    kern = functools.partial(_tailpost_kernel, qr=qr, kvr=kvr, rope=rope, hf=hf)
    return pl.pallas_call(
        kern,
        name="tailpost",
        grid=(tp // tm,),
        in_specs=[
            pl.BlockSpec((tm, ntail), lambda i: (i, 0)),
            pl.BlockSpec((None, 1, qr), lambda i: (l, 0, 0)),
            pl.BlockSpec((None, 1, kvr), lambda i: (l, 0, 0)),
            pl.BlockSpec((None, 1, hf), lambda i: (l, 0, 0)),
            pl.BlockSpec((tm, LANES), lambda i: (i, 0)),
            pl.BlockSpec((tm, LANES), lambda i: (i, 0)),
        ],
        out_specs=(
            pl.BlockSpec((tm, qr), lambda i: (i, 0)),
            pl.BlockSpec((tm, kvr), lambda i: (i, 0)),
            pl.BlockSpec((tm, kvr), lambda i: (i, 0)),
            pl.BlockSpec((tm, rope), lambda i: (i, 0)),
            pl.BlockSpec((tm, hf), lambda i: (i, 0)),
        ),
        out_shape=(
            jax.ShapeDtypeStruct((tp, qr), BF16),
            jax.ShapeDtypeStruct((tp, kvr), F32),
            jax.ShapeDtypeStruct((tp, kvr), BF16),
            jax.ShapeDtypeStruct((tp, rope), F32),
            jax.ShapeDtypeStruct((tp, hf), F32),
        ),
        compiler_params=_cparams(("parallel",), 32),
    )(tail, gq3, gkv3, bf3, cos, sin)


def _qcat_kernel(x_ref, w_ref, cos_ref, sin_ref, o_ref, *, half, qscale):
    acc = _dot(x_ref[...], w_ref[...]) * qscale
    for g in range(acc.shape[1] // LANES):
        slab = acc[:, g * LANES:(g + 1) * LANES]
        if g % 2 == 1:
            slab = _rope128(slab, cos_ref[...], sin_ref[...], half)
        o_ref[:, g * LANES:(g + 1) * LANES] = slab.astype(o_ref.dtype)


def _qcat(cqn, w_uq_cat, cos, sin, l, cfg):
    tp, qr = cqn.shape
    n = w_uq_cat.shape[-1]
    tm = cfg["tm_mm"]
    tn = _pick_tile(n, 512, 2 * LANES)
    kern = functools.partial(_qcat_kernel, half=cfg["ROPE"] // 2,
                             qscale=LOG2E / math.sqrt(cfg["NOPE"] + cfg["ROPE"]))
    return pl.pallas_call(
        kern,
        name="qcat",
        grid=(tp // tm, n // tn),
        in_specs=[
            pl.BlockSpec((tm, qr), lambda i, j: (i, 0)),
            pl.BlockSpec((None, qr, tn), lambda i, j: (l, 0, j)),
            pl.BlockSpec((tm, LANES), lambda i, j: (i, 0)),
            pl.BlockSpec((tm, LANES), lambda i, j: (i, 0)),
        ],
        out_specs=pl.BlockSpec((tm, tn), lambda i, j: (i, j)),
        out_shape=jax.ShapeDtypeStruct((tp, n), BF16),
        compiler_params=_cparams(("parallel", "arbitrary"), 40),
    )(cqn, w_uq_cat, cos, sin)


def _kcat_kernel(x_ref, kr_ref, wk_ref, wv_ref, k_ref, v_ref, *, rope):
    x = x_ref[...]
    if x.dtype != BF16:
        x = x.astype(BF16)
    kn = _dot(x, wk_ref[...])
    v_ref[...] = _dot(x, wv_ref[...]).astype(v_ref.dtype)
    tm = x.shape[0]
    kr = jnp.concatenate([kr_ref[...], jnp.zeros((tm, LANES - rope), F32)], axis=1).astype(BF16)
    for h in range(kn.shape[1] // LANES):
        k_ref[:, (2 * h) * LANES:(2 * h + 1) * LANES] = kn[:, h * LANES:(h + 1) * LANES].astype(BF16)
        k_ref[:, (2 * h + 1) * LANES:(2 * h + 2) * LANES] = kr


def _kcat(ckv, kr, wk, wv, l, cfg, tm):
    rows = ckv.shape[0]
    x_spec = pl.BlockSpec((tm, cfg["KVR"]), lambda i, j: (i, 0))
    kr_spec = pl.BlockSpec((tm, cfg["ROPE"]), lambda i, j: (i, 0))
    hm = cfg["HM"]
    kern = functools.partial(_kcat_kernel, rope=cfg["ROPE"])
    return pl.pallas_call(
        kern,
        name="kcat",
        grid=(rows // tm, hm // 2),
        in_specs=[
            x_spec, kr_spec,
            pl.BlockSpec((None, cfg["KVR"], 2 * LANES), lambda i, j: (l, 0, j)),
            pl.BlockSpec((None, cfg["KVR"], 2 * LANES), lambda i, j: (l, 0, j)),
        ],
        out_specs=(pl.BlockSpec((tm, 4 * LANES), lambda i, j: (i, j)),
                   pl.BlockSpec((tm, 2 * LANES), lambda i, j: (i, j))),
        out_shape=(jax.ShapeDtypeStruct((rows, hm * 2 * LANES), BF16),
                   jax.ShapeDtypeStruct((rows, hm * LANES), BF16)),
        compiler_params=_cparams(("parallel", "arbitrary"), 40),
    )(ckv, kr, wk, wv)


def _cumsum_kernel(x_ref, o_ref, carry_ref, *, out_scale):
    j = pl.program_id(1)

    @pl.when(j == 0)
    def _():
        carry_ref[...] = jnp.zeros_like(carry_ref)

    x = x_ref[...]
    tb = x.shape[1]
    r = lax.broadcasted_iota(I32, (tb, tb), 0)
    c = lax.broadcasted_iota(I32, (tb, tb), 1)
    upper = jnp.where(r <= c, 1.0, 0.0).astype(BF16)
    x1 = x.astype(BF16)
    r1 = x - x1.astype(F32)
    x2 = r1.astype(BF16)
    x3 = (r1 - x2.astype(F32)).astype(BF16)
    cum = _dot(x1, upper) + _dot(x2, upper) + _dot(x3, upper) + carry_ref[...]
    o_ref[...] = cum * out_scale
    carry_ref[...] = cum[:, tb - 1:tb]


def _cumsum(x, out_scale):
    r, h, tl = x.shape
    tb = _pick_tile(tl, 512, LANES)
    return pl.pallas_call(
        functools.partial(_cumsum_kernel, out_scale=out_scale),
        name="cumsum",
        grid=(r, tl // tb),
        in_specs=[pl.BlockSpec((None, h, tb), lambda i, j: (i, 0, j))],
        out_specs=pl.BlockSpec((None, h, tb), lambda i, j: (i, 0, j)),
        out_shape=jax.ShapeDtypeStruct((r, h, tl), F32),
        scratch_shapes=[pltpu.VMEM((h, 1), F32)],
        compiler_params=_cparams(("parallel", "arbitrary"), 32),
    )(x)


def _sb_chunk(q, k_c, v_c, carry, acc, before, upper):
    tq = q.shape[0]
    z = _dot_nt(q, k_c)
    sp = _softplus(z)
    log1m = -sp if before is None else jnp.where(before, -sp, 0.0)
    hi, lo = _split2(log1m)
    cs = _dot(jnp.concatenate([hi, lo], axis=0), upper)
    suffix = cs[:tq] + cs[tq:] + carry
    w = jnp.exp(z - sp + suffix)
    if before is not None:
        w = jnp.where(before, w, 0.0)
    acc = acc + _dot(w.astype(BF16), v_c)
    carry = carry + jnp.sum(log1m, axis=1, keepdims=True)
    return carry, acc


def _strict_upper(n=LANES):
    r = lax.broadcasted_iota(I32, (n, n), 0)
    c = lax.broadcasted_iota(I32, (n, n), 1)
    return jnp.where(r > c, 1.0, 0.0).astype(BF16)


def _sb_prompt_kernel(q_ref, k_ref, v_ref, oin_ref, o_ref, acc_ref, car_ref, *, tq, nh):
    del oin_ref
    qi = pl.program_id(1)
    upper = _strict_upper()
    nch = tq // LANES
    acc_ref[...] = jnp.zeros_like(acc_ref)
    car_ref[...] = jnp.zeros_like(car_ref)
    row = lax.broadcasted_iota(I32, (tq, LANES), 0)
    col = lax.broadcasted_iota(I32, (tq, LANES), 1)

    def visit(base, c, masked):
        off = pl.multiple_of(base + c * LANES, LANES)
        before = (col + c * LANES < row) if masked else None
        for h in range(nh):
            hc = _head_cols(h)
            carry, acc = _sb_chunk(q_ref[:, hc], k_ref[pl.ds(off, LANES), hc], v_ref[pl.ds(off, LANES), hc],
                                   car_ref[h], acc_ref[h], before, upper)
            car_ref[h] = carry
            acc_ref[h] = acc

    for c in reversed(range(nch)):
        visit(qi * tq, c, True)

    def cond(s):
        kb, dead = s
        return jnp.logical_and(kb >= 0, dead == 0)

    def body(s):
        kb, _ = s
        for c in reversed(range(nch)):
            visit(kb * tq, c, False)
        dead = (jnp.max(car_ref[...]) < SB_DEAD).astype(I32)
        return kb - 1, dead

    lax.while_loop(cond, body, (qi - 1, jnp.int32(0)))
    for h in range(nh):
        o_ref[:, _head_cols(h)] = acc_ref[h].astype(o_ref.dtype)


def _pad_rows(x, n):
    return jnp.concatenate([x, jnp.zeros((n - x.shape[0], x.shape[1]), x.dtype)], axis=0)


def _head_cols(h):
    return slice(h * HEAD_DIM, (h + 1) * HEAD_DIM)


def _sb_dec_kernel(q_ref, kn_ref, vn_ref, kc_ref, vc_ref, oin_ref, o_ref, *, ts, p, nh):
    del oin_ref
    upper = _strict_upper()
    cw = 512 if p % 512 == 0 else LANES
    upper_c = _strict_upper(cw)
    row = lax.broadcasted_iota(I32, (ts, LANES), 0)
    col = lax.broadcasted_iota(I32, (ts, LANES), 1)
    for h in range(nh):
        hc = _head_cols(h)
        q = q_ref[:, hc]
        carry = jnp.zeros((ts, 1), F32)
        acc = jnp.zeros((ts, HEAD_DIM), F32)
        carry, acc = _sb_chunk(q, _pad_rows(kn_ref[:, hc], LANES), _pad_rows(vn_ref[:, hc], LANES),
                               carry, acc, col < row, upper)
        for c in reversed(range(p // cw)):
            k_c = kc_ref[c * cw:(c + 1) * cw, h, :].astype(BF16)
            v_c = vc_ref[c * cw:(c + 1) * cw, h, :].astype(BF16)
            carry, acc = _sb_chunk(q, k_c, v_c, carry, acc, None, upper_c)
        o_ref[:, hc] = acc.astype(o_ref.dtype)


PV_CHUNK = 256


def _ones_lane0(rows):
    lane = lax.broadcasted_iota(I32, (rows, LANES), 1)
    return jnp.where(lane == 0, 1.0, 0.0).astype(BF16)


def _flash_block(s, v_of, m_ref, acc_ref):
    tk = s.shape[1]
    m_prev = m_ref[...]
    m_new = jnp.maximum(m_prev, jnp.max(s, axis=1, keepdims=True))
    acc = jnp.exp2(m_prev - m_new) * acc_ref[...]
    ones = _ones_lane0(PV_CHUNK)
    for c in range(tk // PV_CHUNK):
        p = jnp.exp2(s[:, c * PV_CHUNK:(c + 1) * PV_CHUNK] - m_new).astype(BF16)
        acc = acc + _dot(p, jnp.concatenate([v_of(c), ones], axis=1))
    acc_ref[...] = acc
    m_ref[...] = m_new


def _chunk_id(x):
    return lax.shift_right_logical(x, int(math.log2(CHUNK)))


def _sm_prompt_kernel(*refs, tq, fox, nh):
    if fox:
        q_ref, k_ref, v_ref, ck_ref, oin_ref, o_ref, m_ref, acc_ref = refs
    else:
        q_ref, k_ref, v_ref, oin_ref, o_ref, m_ref, acc_ref = refs
    del oin_ref
    qi = pl.program_id(1)
    kw = k_ref.shape[1] // nh
    m_ref[...] = jnp.full_like(m_ref, NEG_BIG)
    acc_ref[...] = jnp.zeros_like(acc_ref)

    def block(kb, masked):
        off = pl.multiple_of(kb * tq, tq)
        for h in range(nh):
            s = _dot_nt(q_ref[:, h * kw:(h + 1) * kw], k_ref[pl.ds(off, tq), h * kw:(h + 1) * kw])
            if fox:
                s = s - ck_ref[h, :, pl.ds(off, tq)]
            if masked:
                r = lax.broadcasted_iota(I32, (tq, tq), 0)
                c = lax.broadcasted_iota(I32, (tq, tq), 1)
                valid = (c <= r) if fox else (_chunk_id(c) <= _chunk_id(r))
                s = jnp.where(valid, s, NEG_BIG)

            def v_of(c, h=h):
                return v_ref[pl.ds(pl.multiple_of(off + c * PV_CHUNK, PV_CHUNK), PV_CHUNK), _head_cols(h)]

            _flash_block(s, v_of, m_ref.at[h], acc_ref.at[h])

    def body(kb, carry):
        block(kb, False)
        return carry

    lax.fori_loop(0, qi, body, 0)
    block(qi, True)
    for h in range(nh):
        acc = acc_ref[h]
        o_ref[:, _head_cols(h)] = (acc[:, :HEAD_DIM] / acc[:, HEAD_DIM:HEAD_DIM + 1]).astype(o_ref.dtype)


def _sm_dec_head(q, kn, vn, kc, vc, ck, ts, p, qpos0):
    s_c = _dot_nt(q, kc)
    s_n = _dot_nt(q, _pad_rows(kn, LANES))
    r = lax.broadcasted_iota(I32, (ts, LANES), 0)
    c = lax.broadcasted_iota(I32, (ts, LANES), 1)
    if ck is not None:
        s_c = s_c - ck[:, :p]
        s_n = s_n - ck[:, p:p + LANES]
        valid = c <= r
    else:
        valid = jnp.logical_and(_chunk_id(c + qpos0) <= _chunk_id(r + qpos0), c < ts)
    s_n = jnp.where(valid, s_n, NEG_BIG)
    m = jnp.maximum(jnp.max(s_c, axis=1, keepdims=True), jnp.max(s_n, axis=1, keepdims=True))
    p_c = jnp.exp2(s_c - m)
    p_n = jnp.exp2(s_n - m)
    den = jnp.sum(p_c, axis=1, keepdims=True) + jnp.sum(p_n, axis=1, keepdims=True)
    num = _dot(p_c.astype(BF16), vc) + _dot(p_n.astype(BF16), _pad_rows(vn, LANES))
    return num / den


def _fox_dec_kernel(q_ref, kn_ref, vn_ref, kc_ref, vc_ref, ck_ref, oin_ref, o_ref, *, ts, p, nh):
    del oin_ref
    for h in range(nh):
        hc = _head_cols(h)
        o = _sm_dec_head(q_ref[:, hc], kn_ref[:, hc], vn_ref[:, hc],
                         kc_ref[:, h, :].astype(BF16), vc_ref[:, h, :].astype(BF16),
                         ck_ref[h], ts, p, p)
        o_ref[:, hc] = o.astype(o_ref.dtype)


def _mla_dec_kernel(q_ref, cn_ref, rn_ref, cc_ref, rc_ref, wk_ref, wv_ref, oin_ref, o_ref, *, ts, p, nh, rope):
    del oin_ref
    f = 2 * LANES
    q_lat, q_rope = [], []
    for h in range(nh):
        q_lat.append(_dot_nt(q_ref[:, h * f:h * f + LANES], wk_ref[:, _head_cols(h)]).astype(BF16))
        q_rope.append(q_ref[:, h * f + LANES:(h + 1) * f])
    q_lat = jnp.concatenate(q_lat, axis=0)
    q_rope = jnp.concatenate(q_rope, axis=0)
    rows = nh * ts

    def rope_slab(r):
        return jnp.concatenate([r, jnp.zeros((r.shape[0], LANES - rope), F32)], axis=1).astype(BF16)

    cc = cc_ref[...].astype(BF16)
    cn = _pad_rows(cn_ref[...], LANES)
    s_c = _dot_nt(q_lat, cc) + _dot_nt(q_rope, rope_slab(rc_ref[...]))
    s_n = _dot_nt(q_lat, cn) + _dot_nt(q_rope, _pad_rows(rope_slab(rn_ref[...]), LANES))
    tok = lax.rem(lax.broadcasted_iota(I32, (rows, LANES), 0), ts)
    c = lax.broadcasted_iota(I32, (rows, LANES), 1)
    valid = jnp.logical_and(_chunk_id(c + p) <= _chunk_id(tok + p), c < ts)
    s_n = jnp.where(valid, s_n, NEG_BIG)
    m = jnp.maximum(jnp.max(s_c, axis=1, keepdims=True), jnp.max(s_n, axis=1, keepdims=True))
    p_c = jnp.exp2(s_c - m)
    p_n = jnp.exp2(s_n - m)
    den = jnp.sum(p_c, axis=1, keepdims=True) + jnp.sum(p_n, axis=1, keepdims=True)
    o_lat = ((_dot(p_c.astype(BF16), cc) + _dot(p_n.astype(BF16), cn)) / den).astype(BF16)
    for h in range(nh):
        o = _dot(o_lat[h * ts:(h + 1) * ts, :], wv_ref[:, _head_cols(h)])
        o_ref[:, _head_cols(h)] = o.astype(o_ref.dtype)


def _o_specs(o_prev, tp, dmix):
    shape = jax.ShapeDtypeStruct((tp, dmix), BF16)
    if o_prev is None:
        o_prev = jnp.zeros((8, LANES), BF16)
        return o_prev, shape, {}
    return o_prev, shape, None


def _attn_call(kern, grid, in_specs, args, o_prev, o_spec, tp, dmix, scratch, vmem, n_in):
    o_arg, o_shape, alias = _o_specs(o_prev, tp, dmix)
    aliases = {} if alias is not None else {n_in: 0}
    return pl.pallas_call(
        kern,
        name=kern.func.__name__.strip("_").replace("_kernel", "")
        + {True: "_fox", False: "_mla", None: ""}[kern.keywords.get("fox")],
        grid=grid,
        in_specs=in_specs + [pl.BlockSpec(memory_space=pl.ANY)],
        out_specs=o_spec,
        out_shape=o_shape,
        scratch_shapes=scratch,
        input_output_aliases=aliases,
        compiler_params=_cparams(("parallel", "arbitrary"), vmem),
    )(*args, o_arg)


def _cache_spec(l, p, nh):
    return pl.BlockSpec((None, None, p, nh, LANES), lambda b, _: (l, b, 0, 0, 0))


def _dec_specs(rb, ts, nh, cq0, ck0, cv0, oc0):
    w = nh * LANES
    assert cq0 % nh == 0 and ck0 % nh == 0 and cv0 % nh == 0 and oc0 % nh == 0
    ins = [pl.BlockSpec((ts, w), lambda b, _, c=c0 // nh: (rb + b, c)) for c0 in (cq0, ck0, cv0)]
    return ins, pl.BlockSpec((ts, w), lambda b, _: (rb + b, oc0 // nh))


def _sb_prompt(qkv, o_prev, cfg, cq0, ck0, cv0, oc0):
    t, tp, hs = cfg["T"], cfg["Tp"], cfg["HS"]
    tq = cfg["tq_sb"]
    nh = SM_HEADS
    assert hs % nh == 0 and cq0 % nh == 0 and ck0 % nh == 0 and cv0 % nh == 0 and oc0 % nh == 0
    w = nh * LANES
    kern = functools.partial(_sb_prompt_kernel, tq=tq, nh=nh)
    in_specs = [
        pl.BlockSpec((tq, w), lambda h, i: (i, cq0 // nh + h)),
        pl.BlockSpec((t, w), lambda h, i: (0, ck0 // nh + h)),
        pl.BlockSpec((t, w), lambda h, i: (0, cv0 // nh + h)),
    ]
    o_spec = pl.BlockSpec((tq, w), lambda h, i: (i, oc0 // nh + h))
    scratch = [pltpu.VMEM((nh, tq, HEAD_DIM), F32), pltpu.VMEM((nh, tq, 1), F32)]
    return _attn_call(kern, (hs // nh, t // tq), in_specs, (qkv, qkv, qkv), o_prev, o_spec, tp, cfg["DMIX"],
                      scratch, 40, 3)


def _sb_dec(qkv, kc, vc, l, o_prev, cfg, cq0, ck0, cv0, oc0):
    t, tp, ts, bs, p, hs = cfg["T"], cfg["Tp"], cfg["Ts"], cfg["Bs"], cfg["P"], cfg["HS"]
    rb = t // ts
    kern = functools.partial(_sb_dec_kernel, ts=ts, p=p, nh=hs)
    in_specs, o_spec = _dec_specs(rb, ts, hs, cq0, ck0, cv0, oc0)
    in_specs += [_cache_spec(l, p, hs), _cache_spec(l, p, hs)]
    return _attn_call(kern, (bs, 1), in_specs, (qkv, qkv, qkv, kc, vc), o_prev, o_spec, tp, cfg["DMIX"],
                      [], 40, 5)


SM_HEADS = 2


def _sm_scratch(tq, nh):
    return [pltpu.VMEM((nh, tq, 1), F32), pltpu.VMEM((nh, tq, 2 * HEAD_DIM), F32)]


def _fox_prompt(qkv, cum_row, o_prev, cfg, cq0, ck0, cv0, oc0):
    t, tp, hf = cfg["T"], cfg["Tp"], cfg["HF"]
    tq = cfg["tq_sm"]
    nh = SM_HEADS
    assert hf % nh == 0 and cq0 % nh == 0 and ck0 % nh == 0 and cv0 % nh == 0 and oc0 % nh == 0
    w = nh * LANES
    kern = functools.partial(_sm_prompt_kernel, tq=tq, fox=True, nh=nh)
    in_specs = [
        pl.BlockSpec((tq, w), lambda h, i: (i, cq0 // nh + h)),
        pl.BlockSpec((t, w), lambda h, i: (0, ck0 // nh + h)),
        pl.BlockSpec((t, w), lambda h, i: (0, cv0 // nh + h)),
        pl.BlockSpec((nh, 1, t), lambda h, i: (h, 0, 0)),
    ]
    o_spec = pl.BlockSpec((tq, w), lambda h, i: (i, oc0 // nh + h))
    return _attn_call(kern, (hf // nh, t // tq), in_specs, (qkv, qkv, qkv, cum_row), o_prev, o_spec,
                      tp, cfg["DMIX"], _sm_scratch(tq, nh), 56, 4)


def _fox_dec(qkv, kc, vc, ck_s, l, o_prev, cfg, cq0, ck0, cv0, oc0):
    t, tp, ts, bs, p, hf = cfg["T"], cfg["Tp"], cfg["Ts"], cfg["Bs"], cfg["P"], cfg["HF"]
    rb = t // ts
    kern = functools.partial(_fox_dec_kernel, ts=ts, p=p, nh=hf)
    in_specs, o_spec = _dec_specs(rb, ts, hf, cq0, ck0, cv0, oc0)
    in_specs += [_cache_spec(l, p, hf), _cache_spec(l, p, hf),
                 pl.BlockSpec((None, hf, 1, p + LANES), lambda b, _: (b, 0, 0, 0))]
    return _attn_call(kern, (bs, 1), in_specs, (qkv, qkv, qkv, kc, vc, ck_s), o_prev, o_spec, tp,
                      cfg["DMIX"], [], 40, 6)


def _mla_prompt(qcat, kcat, vm, o_prev, cfg, oc0):
    t, tp, hm = cfg["T"], cfg["Tp"], cfg["HM"]
    tq = cfg["tq_sm"]
    nh = SM_HEADS
    assert hm % nh == 0 and oc0 % nh == 0
    kern = functools.partial(_sm_prompt_kernel, tq=tq, fox=False, nh=nh)
    in_specs = [
        pl.BlockSpec((tq, nh * 2 * LANES), lambda h, i: (i, h)),
        pl.BlockSpec((t, nh * 2 * LANES), lambda h, i: (0, h)),
        pl.BlockSpec((t, nh * LANES), lambda h, i: (0, h)),
    ]
    o_spec = pl.BlockSpec((tq, nh * LANES), lambda h, i: (i, oc0 // nh + h))
    return _attn_call(kern, (hm // nh, t // tq), in_specs, (qcat, kcat, vm), o_prev, o_spec, tp, cfg["DMIX"],
                      _sm_scratch(tq, nh), 56, 3)


def _mla_dec(qcat, ckvb, kr, ckv_cache, kr_cache, wk, wv, l, o_prev, cfg, oc0):
    t, tp, ts, bs, p, hm = cfg["T"], cfg["Tp"], cfg["Ts"], cfg["Bs"], cfg["P"], cfg["HM"]
    kvr, rope = cfg["KVR"], cfg["ROPE"]
    rb = t // ts
    assert oc0 % hm == 0
    kern = functools.partial(_mla_dec_kernel, ts=ts, p=p, nh=hm, rope=rope)
    in_specs = [
        pl.BlockSpec((ts, hm * 2 * LANES), lambda b, _: (rb + b, 0)),
        pl.BlockSpec((ts, kvr), lambda b, _: (rb + b, 0)),
        pl.BlockSpec((ts, rope), lambda b, _: (rb + b, 0)),
        pl.BlockSpec((None, None, p, kvr), lambda b, _: (l, b, 0, 0)),
        pl.BlockSpec((None, None, p, rope), lambda b, _: (l, b, 0, 0)),
        pl.BlockSpec((None, kvr, hm * LANES), lambda b, _: (l, 0, 0)),
        pl.BlockSpec((None, kvr, hm * LANES), lambda b, _: (l, 0, 0)),
    ]
    o_spec = pl.BlockSpec((ts, hm * LANES), lambda b, _: (rb + b, oc0 // hm))
    return _attn_call(kern, (bs, 1), in_specs, (qcat, ckvb, kr, ckv_cache, kr_cache, wk, wv), o_prev,
                      o_spec, tp, cfg["DMIX"], [], 40, 7)


def _gnwo_kernel(o_ref, g_ref, w_ref, x_ref, gt_ref, yin_ref, y_ref, on_ref, *, bounds, sample, nb, ts, prow):
    del yin_ref
    j = pl.program_id(1)

    @pl.when(j == 0)
    def _():
        for lo, hi in bounds:
            part = o_ref[:, lo:hi].astype(F32)
            on_ref[:, lo:hi] = (_rms(part) * g_ref[:, lo:hi]).astype(BF16)

    mix = _dot(on_ref[...], w_ref[...].astype(BF16))
    gt = _expand_rows(gt_ref[0:nb, :], ts) if sample else gt_ref[prow:prow + 1, :]
    y_ref[...] = x_ref[...] + gt * mix


def _gnwo_part(o, g3, w_o, x, mods, l, cfg, sample, y_prev):
    tp, dmix = o.shape
    d = x.shape[1]
    t, bs, ts = cfg["T"], cfg["Bs"], cfg["Ts"]
    tm = bs * ts if sample else _pick_tile(t, 1024, 16)
    rb0 = t // tm if sample else 0
    rows = bs * ts if sample else t
    tn = _pick_tile(d, 512, LANES)
    bounds = ((0, cfg["DSB"]), (cfg["DSB"], cfg["DSB"] + cfg["DFX"]), (cfg["DSB"] + cfg["DFX"], dmix))
    kern = functools.partial(_gnwo_kernel, bounds=bounds, sample=sample, nb=bs, ts=ts, prow=cfg["prow"])
    if y_prev is None:
        y_prev, aliases = jnp.zeros((8, LANES), F32), {}
    else:
        aliases = {5: 0}
    return pl.pallas_call(
        kern,
        name="gnwo",
        grid=(rows // tm, d // tn),
        in_specs=[
            pl.BlockSpec((tm, dmix), lambda i, j: (i + rb0, 0)),
            pl.BlockSpec((None, 1, dmix), lambda i, j: (l, 0, 0)),
            pl.BlockSpec((None, dmix, tn), lambda i, j: (l, 0, j)),
            pl.BlockSpec((tm, tn), lambda i, j: (i + rb0, j)),
            pl.BlockSpec((None, MODROWS, tn), lambda i, j: (l, 0, 2 * (d // tn) + j)),
            pl.BlockSpec(memory_space=pl.ANY),
        ],
        out_specs=pl.BlockSpec((tm, tn), lambda i, j: (i + rb0, j)),
        out_shape=jax.ShapeDtypeStruct((tp, d), F32),
        scratch_shapes=[pltpu.VMEM((tm, dmix), BF16)],
        input_output_aliases=aliases,
        compiler_params=_cparams(("parallel", "arbitrary"), 56),
    )(o, g3, w_o, x, mods, y_prev)


def _gnwo(o, g3, w_o, x, mods, l, cfg):
    y = _gnwo_part(o, g3, w_o, x, mods, l, cfg, False, None)
    return _gnwo_part(o, g3, w_o, x, mods, l, cfg, True, y)


def _pack_pair(a, b):
    ua = lax.bitcast_convert_type(a.astype(BF16).astype(F32), U32)
    ub = lax.bitcast_convert_type(b.astype(BF16).astype(F32), U32)
    return (ua & jnp.uint32(0xFFFF0000)) | lax.shift_right_logical(ub, jnp.uint32(16))


def _unpack_pair(u):
    hi = lax.bitcast_convert_type(u & jnp.uint32(0xFFFF0000), F32)
    lo = lax.bitcast_convert_type(lax.shift_left(u, jnp.uint32(16)), F32)
    return hi, lo


def _moepre_kernel(x_ref, g_ref, sc_ref, sh_ref, wr_ref, br_ref,
                   hp_ref, idx_ref, gate_ref, pos_ref, cnt_ref, carry_ref, *, npt, tm, ts, prow):
    i = pl.program_id(0)

    @pl.when(i == 0)
    def _():
        carry_ref[...] = jnp.zeros_like(carry_ref)

    xn = _rms(x_ref[...]) * g_ref[...]
    d = xn.shape[1]

    def emit(sc, sh):
        h = xn * (1.0 + sc) + sh
        hp_ref[...] = _pack_pair(h[:, :d // 2], h[:, d // 2:])
        h1, h2 = _split2(h)
        w = wr_ref[...]
        w1, w2 = _split2(w)
        logits = _dot(h1, w1) + (_dot(h1, w2) + _dot(h2, w1)) + br_ref[...]
        lane = lax.broadcasted_iota(I32, (tm, LANES), 1).astype(F32)
        picks, vals, sels = [], [], []
        work = logits
        for _ in range(TOP_K):
            m = jnp.max(work, axis=1, keepdims=True)
            sel = jnp.min(jnp.where(work == m, lane, float(LANES)), axis=1, keepdims=True)
            hit = lane == sel
            picks.append(hit)
            vals.append(m)
            sels.append(sel)
            work = jnp.where(hit, -jnp.inf, work)
        es = [jnp.exp(v - vals[0]) for v in vals]
        den = es[0] + es[1] + es[2] + es[3]
        onehot = jnp.zeros((tm, LANES), F32)
        for hit in picks:
            onehot = onehot + jnp.where(hit, 1.0, 0.0)
        r = lax.broadcasted_iota(I32, (tm, tm), 0)
        c = lax.broadcasted_iota(I32, (tm, tm), 1)
        lower = jnp.where(c < r, 1.0, 0.0).astype(BF16)
        rank = _dot(lower, onehot.astype(BF16)) + carry_ref[...]
        for k in range(TOP_K):
            idx_ref[:, k:k + 1] = sels[k].astype(I32)
            gate_ref[:, k:k + 1] = es[k] / den
            pos_ref[:, k:k + 1] = jnp.sum(jnp.where(picks[k], rank, 0.0), axis=1,
                                          keepdims=True).astype(I32)
        total = carry_ref[...] + jnp.sum(onehot, axis=0, keepdims=True)
        carry_ref[...] = total
        cnt_ref[...] = total.astype(I32)

    _modulated(i, npt, tm, ts, prow, (sc_ref, sh_ref), emit)


def _moepre(x, g3, mods, w_router, b3, l, cfg):
    tp, d = x.shape
    tm = cfg["tm_row"]
    kern = functools.partial(_moepre_kernel, npt=cfg["T"] // tm, tm=tm, ts=cfg["Ts"], prow=cfg["prow"])
    return pl.pallas_call(
        kern,
        name="moepre",
        grid=(tp // tm,),
        in_specs=[
            pl.BlockSpec((tm, d), lambda i: (i, 0)),
            pl.BlockSpec((None, 1, d), lambda i: (l, 0, 0)),
            pl.BlockSpec((None, MODROWS, d), lambda i: (l, 0, 4)),
            pl.BlockSpec((None, MODROWS, d), lambda i: (l, 0, 3)),
            pl.BlockSpec((None, d, LANES), lambda i: (l, 0, 0)),
            pl.BlockSpec((None, 1, LANES), lambda i: (l, 0, 0)),
        ],
        out_specs=(
            pl.BlockSpec((tm, d // 2), lambda i: (i, 0)),
            pl.BlockSpec((tm, TOP_K), lambda i: (i, 0)),
            pl.BlockSpec((tm, TOP_K), lambda i: (i, 0)),
            pl.BlockSpec((tm, TOP_K), lambda i: (i, 0)),
            pl.BlockSpec((1, LANES), lambda i: (0, 0)),
        ),
        out_shape=(
            jax.ShapeDtypeStruct((tp, d // 2), U32),
            jax.ShapeDtypeStruct((tp, TOP_K), I32),
            jax.ShapeDtypeStruct((tp, TOP_K), F32),
            jax.ShapeDtypeStruct((tp, TOP_K), I32),
            jax.ShapeDtypeStruct((1, LANES), I32),
        ),
        scratch_shapes=[pltpu.VMEM((1, LANES), F32)],
        compiler_params=_cparams(("arbitrary",), 40),
    )(x, g3, mods, mods, w_router, b3)


def _slot_token(a):
    return lax.shift_right_logical(a, int(math.log2(TOP_K)))


def _slot_choice(a):
    return lax.bitwise_and(a, TOP_K - 1)


def _dispatch_kernel(zrow_ref, dest_ref, hp_ref, xs_ref, zero_ref, zsem, sem, *, td, ne):
    i = pl.program_id(0)

    def zero_copy(e):
        r0 = pl.multiple_of(zrow_ref[e], MOE_TILE)
        return pltpu.make_async_copy(zero_ref, xs_ref.at[pl.ds(r0, MOE_TILE)], zsem)

    @pl.when(i == 0)
    def _():
        zero_ref[...] = jnp.zeros_like(zero_ref)
        for e in range(ne):
            zero_copy(e).start()
        for e in range(ne):
            zero_copy(e).wait()

    n = td * TOP_K

    def row_copy(a):
        return pltpu.make_async_copy(hp_ref.at[pl.ds(_slot_token(a), 1)],
                                     xs_ref.at[pl.ds(dest_ref[0, 0, a], 1)], sem)

    def start_pair(a2, c):
        for u in range(2):
            row_copy(2 * a2 + u).start(priority=u)
        return c

    def wait(a, c):
        row_copy(a).wait()
        return c

    lax.fori_loop(0, n // 2, start_pair, 0, unroll=DMA_UNROLL // 2)
    lax.fori_loop(0, n, wait, 0, unroll=DMA_UNROLL)


def _dispatch(hp, dest, zrow, cfg):
    tp, dh = hp.shape
    td = cfg["tm_row"]
    nt = tp // td
    kern = functools.partial(_dispatch_kernel, td=td, ne=cfg["E"])
    grid_spec = pltpu.PrefetchScalarGridSpec(
        num_scalar_prefetch=1,
        grid=(nt,),
        in_specs=[
            pl.BlockSpec((1, 1, td * TOP_K), lambda i, z: (i, 0, 0), memory_space=pltpu.SMEM),
            pl.BlockSpec((td, dh), lambda i, z: (i, 0)),
        ],
        out_specs=pl.BlockSpec(memory_space=pl.ANY),
        scratch_shapes=[pltpu.VMEM((MOE_TILE, dh), U32), pltpu.SemaphoreType.DMA(()),
                        pltpu.SemaphoreType.DMA(())],
    )
    return pl.pallas_call(
        kern,
        name="dispatch",
        grid_spec=grid_spec,
        out_shape=jax.ShapeDtypeStruct((cfg["PA"], dh), U32),
        compiler_params=pltpu.CompilerParams(dimension_semantics=("arbitrary",), vmem_limit_bytes=32 * MIB,
                                             has_side_effects=True),
    )(zrow, dest.reshape(nt, 1, td * TOP_K), hp)


def _tile_is_first(te_ref, i):
    prev = te_ref[jnp.maximum(i - 1, 0)]
    return jnp.logical_or(i == 0, te_ref[i] != prev)


def _g1_kernel(te_ref, nv_ref, xs_ref, wg_ref, bg_ref, wu_ref, bu_ref, act_ref, wgb_ref, wub_ref):
    i = pl.program_id(1)

    @pl.when(i < nv_ref[0])
    def _():
        @pl.when(_tile_is_first(te_ref, i))
        def _():
            wgb_ref[...] = wg_ref[...].astype(BF16)
            wub_ref[...] = wu_ref[...].astype(BF16)

        xa, xb = _unpack_pair(xs_ref[...])
        xa = xa.astype(BF16)
        xb = xb.astype(BF16)
        dh = xa.shape[1]
        g = _dot(xa, wgb_ref[:dh, :]) + _dot(xb, wgb_ref[dh:, :]) + bg_ref[...]
        u = _dot(xa, wub_ref[:dh, :]) + _dot(xb, wub_ref[dh:, :]) + bu_ref[...]
        g = jnp.minimum(g, SWIGLU_LIMIT)
        u = jnp.clip(u, -SWIGLU_LIMIT, SWIGLU_LIMIT)
        act_ref[...] = ((u + 1.0) * g * _sigmoid(SWIGLU_ALPHA * g)).astype(act_ref.dtype)


def _g1(xs, w_gate, b_gate4, w_up, b_up4, te, nv, l, cfg):
    pa, dh = xs.shape
    d, de = w_gate.shape[2], w_gate.shape[3]
    tn = _pick_tile(de, 512, LANES)
    nt = pa // MOE_TILE

    def row(i, te_ref, nv_ref):
        return jnp.minimum(i, nv_ref[0] - 1)

    grid_spec = pltpu.PrefetchScalarGridSpec(
        num_scalar_prefetch=2,
        grid=(de // tn, nt),
        in_specs=[
            pl.BlockSpec((MOE_TILE, dh), lambda j, i, te_ref, nv_ref: (row(i, te_ref, nv_ref), 0)),
            pl.BlockSpec((None, None, d, tn), lambda j, i, te_ref, nv_ref: (l, te_ref[i], 0, j)),
            pl.BlockSpec((None, None, 1, tn), lambda j, i, te_ref, nv_ref: (l, te_ref[i], 0, j)),
            pl.BlockSpec((None, None, d, tn), lambda j, i, te_ref, nv_ref: (l, te_ref[i], 0, j)),
            pl.BlockSpec((None, None, 1, tn), lambda j, i, te_ref, nv_ref: (l, te_ref[i], 0, j)),
        ],
        out_specs=pl.BlockSpec((MOE_TILE, tn), lambda j, i, te_ref, nv_ref: (row(i, te_ref, nv_ref), j)),
        scratch_shapes=[pltpu.VMEM((d, tn), BF16), pltpu.VMEM((d, tn), BF16)],
    )
    return pl.pallas_call(
        _g1_kernel,
        name="expert_up",
        grid_spec=grid_spec,
        out_shape=jax.ShapeDtypeStruct((pa, de), BF16),
        compiler_params=_cparams(("arbitrary", "arbitrary"), 56),
    )(te, nv, xs, w_gate, b_gate4, w_up, b_up4)


def _g2_kernel(te_ref, nv_ref, act_ref, wd_ref, bd_ref, ys_ref, wdb_ref):
    i = pl.program_id(1)

    @pl.when(i < nv_ref[0])
    def _():
        @pl.when(_tile_is_first(te_ref, i))
        def _():
            wdb_ref[...] = wd_ref[...].astype(BF16)

        y = _dot(act_ref[...], wdb_ref[...]) + bd_ref[...]
        half = y.shape[1] // 2
        ys_ref[...] = _pack_pair(y[:, :half], y[:, half:])


def _g2(act, w_down, b_down4, te, nv, l, cfg):
    pa, de = act.shape
    d = w_down.shape[3]
    tn = _pick_tile(d, 2048, 2 * LANES)
    nt = pa // MOE_TILE

    def row(i, te_ref, nv_ref):
        return jnp.minimum(i, nv_ref[0] - 1)

    grid_spec = pltpu.PrefetchScalarGridSpec(
        num_scalar_prefetch=2,
        grid=(d // tn, nt),
        in_specs=[
            pl.BlockSpec((MOE_TILE, de), lambda j, i, te_ref, nv_ref: (row(i, te_ref, nv_ref), 0)),
            pl.BlockSpec((None, None, de, tn), lambda j, i, te_ref, nv_ref: (l, te_ref[i], 0, j)),
            pl.BlockSpec((None, None, 1, tn), lambda j, i, te_ref, nv_ref: (l, te_ref[i], 0, j)),
        ],
        out_specs=pl.BlockSpec((MOE_TILE, tn // 2), lambda j, i, te_ref, nv_ref: (row(i, te_ref, nv_ref), j)),
        scratch_shapes=[pltpu.VMEM((de, tn), BF16)],
    )
    return pl.pallas_call(
        _g2_kernel,
        name="expert_down",
        grid_spec=grid_spec,
        out_shape=jax.ShapeDtypeStruct((pa, d // 2), U32),
        compiler_params=_cparams(("arbitrary", "arbitrary"), 48),
    )(te, nv, act, w_down, b_down4), tn


def _combine_kernel(dcur_ref, dnxt_ref, ys_ref, gate_ref, x_ref, gt_ref, o_ref, buf_ref, sem,
                    *, tc, nt, tn, npt, ts, prow):
    i = pl.program_id(0)
    slot = lax.rem(i, 2)
    n = tc * TOP_K

    def row_copy(d_ref, s, a):
        return pltpu.make_async_copy(ys_ref.at[pl.ds(d_ref[0, 0, a], 1)],
                                     buf_ref.at[s, _slot_choice(a), pl.ds(_slot_token(a), 1)], sem.at[s])

    def issue(d_ref, s):
        def pair(a2, c):
            for u in range(2):
                row_copy(d_ref, s, 2 * a2 + u).start(priority=u)
            return c

        lax.fori_loop(0, n // 2, pair, 0, unroll=DMA_UNROLL // 2)

    @pl.when(i == 0)
    def _():
        issue(dcur_ref, 0)

    @pl.when(i + 1 < nt)
    def _():
        issue(dnxt_ref, 1 - slot)

    lax.fori_loop(0, n, lambda a, c: (row_copy(dcur_ref, slot, a).wait(), c)[1], 0, unroll=DMA_UNROLL)

    gate = gate_ref[...]
    half = tn // 2
    dh = buf_ref.shape[-1]
    parts = [None] * (2 * (dh // half))
    for k in range(TOP_K):
        hi, lo = _unpack_pair(buf_ref[slot, k])
        gk = gate[:, k:k + 1]
        for j in range(dh // half):
            a = gk * hi[:, j * half:(j + 1) * half]
            b = gk * lo[:, j * half:(j + 1) * half]
            parts[2 * j] = a if parts[2 * j] is None else parts[2 * j] + a
            parts[2 * j + 1] = b if parts[2 * j + 1] is None else parts[2 * j + 1] + b
    moe = jnp.concatenate(parts, axis=1)

    def emit(gt):
        o_ref[...] = x_ref[...] + gt * moe

    _modulated(i, npt, tc, ts, prow, (gt_ref,), emit)


def _combine(ys, dest, gate, x, mods, l, tn, cfg):
    tp, d = x.shape
    tc = cfg["tc"]
    nt = tp // tc
    dh = ys.shape[1]
    kern = functools.partial(_combine_kernel, tc=tc, nt=nt, tn=tn, npt=cfg["T"] // tc, ts=cfg["Ts"],
                             prow=cfg["prow"])
    dest3 = dest.reshape(nt, 1, tc * TOP_K)
    return pl.pallas_call(
        kern,
        name="combine",
        grid=(nt,),
        in_specs=[
            pl.BlockSpec((1, 1, tc * TOP_K), lambda i: (i, 0, 0), memory_space=pltpu.SMEM),
            pl.BlockSpec((1, 1, tc * TOP_K), lambda i: (jnp.minimum(i + 1, nt - 1), 0, 0),
                         memory_space=pltpu.SMEM),
            pl.BlockSpec(memory_space=pl.ANY),
            pl.BlockSpec((tc, TOP_K), lambda i: (i, 0)),
            pl.BlockSpec((tc, d), lambda i: (i, 0)),
            pl.BlockSpec((None, MODROWS, d), lambda i: (l, 0, 5)),
        ],
        out_specs=pl.BlockSpec((tc, d), lambda i: (i, 0)),
        out_shape=jax.ShapeDtypeStruct((tp, d), F32),
        scratch_shapes=[pltpu.VMEM((2, TOP_K, tc, dh), U32), pltpu.SemaphoreType.DMA((2,))],
        compiler_params=_cparams(("arbitrary",), 40),
    )(dest3, dest3, ys, gate, x, mods)


def _moe(x, g3, mods, w_router, b_router3, w_gate, b_gate4, w_up, b_up4, w_down, b_down4, l, cfg):
    ne = cfg["E"]
    hp, idx, gate, pos, cnt = _moepre(x, g3, mods, w_router, b_router3, l, cfg)
    cnt = cnt[0, :ne]
    pc = ((cnt + MOE_TILE - 1) // MOE_TILE) * MOE_TILE
    ends = jnp.cumsum(pc)
    off = ends - pc
    dest = off[idx] + pos
    nt = cfg["PA"] // MOE_TILE
    tile_ends = ends // MOE_TILE
    nv = tile_ends[-1]
    tiles = jnp.minimum(jnp.arange(nt, dtype=I32), nv - 1)
    te = jnp.minimum(jnp.sum((tile_ends[None, :] <= tiles[:, None]).astype(I32), axis=1), ne - 1)
    zrow = jnp.clip(ends - MOE_TILE, 0, cfg["PA"] - MOE_TILE).astype(I32)
    nv1 = nv.reshape(1).astype(I32)
    xs = _dispatch(hp, dest.astype(I32), zrow, cfg)
    act = _g1(xs, w_gate, b_gate4, w_up, b_up4, te, nv1, l, cfg)
    ys, tn = _g2(act, w_down, b_down4, te, nv1, l, cfg)
    return _combine(ys, dest.astype(I32), gate, x, mods, l, tn, cfg)


def _rope_tables(cfg):
    half = cfg["ROPE"] // 2
    inv_freq = ROPE_THETA ** (-jnp.arange(half, dtype=F32) / half)
    pos = jnp.concatenate([jnp.arange(cfg["T"], dtype=I32),
                           jnp.tile(cfg["P"] + jnp.arange(cfg["Ts"], dtype=I32), cfg["Bs"])])
    ang = pos.astype(F32)[:, None] * inv_freq[None, :]
    cos, sin = jnp.cos(ang), jnp.sin(ang)
    pad = jnp.zeros((pos.shape[0], LANES - 2 * half), F32)
    return (jnp.concatenate([cos, cos, pad], axis=1), jnp.concatenate([-sin, sin, pad], axis=1))


def kernel(x_prompt, x_sample, c_prompt, c_sample, cache_sb_k, cache_sb_v, cache_fox_k, cache_fox_v,
           cache_fox_logf, cache_mla_ckv, cache_mla_krope, w_in, b_forget, g_q, w_uq, g_kv, w_uk, w_uv,
           g_out, w_o, g_attn, g_mlp, w_ada, b_ada, w_router, b_router, w_gate, b_gate, w_up, b_up,
           w_down, b_down, g_final, w_final_ada, b_final_ada):
    nl, d, _ = w_in.shape
    bp, t, _ = x_prompt.shape
    bs, ts, _ = x_sample.shape
    p = cache_sb_k.shape[2]
    hs, hf = cache_sb_k.shape[3], cache_fox_k.shape[3]
    kvr, rope = cache_mla_ckv.shape[-1], cache_mla_krope.shape[-1]
    hm, nope, vdim = w_uk.shape[2], w_uk.shape[3], w_uv.shape[3]
    qr = g_q.shape[1]
    ne, de = w_router.shape[-1], w_gate.shape[-1]
    dsb, dfx = hs * HEAD_DIM, hf * HEAD_DIM
    dmix = g_out.shape[1]
    assert bp == 1 and nope == HEAD_DIM and vdim == HEAD_DIM and rope <= LANES // 2 * 2
    assert p % LANES == 0 and ts % 16 == 0 and dmix == dsb + dfx + hm * HEAD_DIM
    tp = t + bs * ts
    tm_row = bs * ts
    assert t % tm_row == 0 and tm_row % 16 == 0
    n_assign = tp * TOP_K
    pa = (-(-(n_assign + ne * (MOE_TILE - 1)) // MOE_TILE)) * MOE_TILE
    cfg = dict(T=t, Tp=tp, Bs=bs, Ts=ts, P=p, HS=hs, HF=hf, HM=hm, KVR=kvr, ROPE=rope, NOPE=nope, QR=qr,
               E=ne, DE=de, DSB=dsb, DFX=dfx, DMIX=dmix, prow=bs, PA=pa,
               tm_row=tm_row, tm_mm=_pick_tile(tp, 1056, 16),
               tq_sb=_pick_tile(t, 256, LANES), tq_sm=_pick_tile(t, 1024, PV_CHUNK),
               tc=_pick_tile(tm_row, 128, ts))
    assert bs + 1 <= MODROWS

    c_all = jnp.concatenate([c_sample, c_prompt, jnp.zeros((MODROWS - bs - 1, d), F32)], axis=0)
    mods = _ada(c_all, w_ada, b_ada)
    modf = _ada(c_all, w_final_ada[None], b_final_ada[None])

    c0 = 3 * dsb + 3 * dfx
    zpad = lambda n: jnp.zeros((nl, d, n), F32)
    w_tail = jnp.concatenate([
        w_in[:, :, c0 + hf:c0 + hf + qr + kvr],
        w_in[:, :, c0 + hf + qr + kvr:], zpad(LANES - rope),
        w_in[:, :, c0:c0 + hf], zpad(LANES - hf)], axis=2)
    ntail = w_tail.shape[2]
    w_uq4 = w_uq.reshape(nl, qr, hm, nope + rope)
    w_uq_cat = jnp.concatenate([w_uq4, jnp.zeros((nl, qr, hm, 2 * LANES - nope - rope), F32)],
                               axis=3).reshape(nl, qr, hm * 2 * LANES).astype(BF16)
    w_ukf = w_uk.reshape(nl, kvr, hm * nope).astype(BF16)
    w_uvf = w_uv.reshape(nl, kvr, hm * vdim).astype(BF16)
    cos, sin = _rope_tables(cfg)

    g_attn3, g_mlp3 = g_attn[:, None, :], g_mlp[:, None, :]
    g_q3, g_kv3, b_f3, g_out3 = g_q[:, None, :], g_kv[:, None, :], b_forget[:, None, :], g_out[:, None, :]
    w_router_p = jnp.concatenate([w_router, jnp.zeros((nl, d, LANES - ne), F32)], axis=2)
    b_router3 = jnp.concatenate([b_router, jnp.full((nl, LANES - ne), -jnp.inf, F32)], axis=1)[:, None, :]
    b_gate4, b_up4, b_down4 = b_gate[:, :, None, :], b_up[:, :, None, :], b_down[:, :, None, :]

    x = jnp.concatenate([x_prompt[0], x_sample.reshape(bs * ts, d)], axis=0)
    tm_mm = cfg["tm_mm"]
    nsb, nfx = dsb // LANES, dfx // LANES
    states = [[] for _ in range(7)]
    for l in range(nl):
        h = _normmod(x, g_attn3, mods, l, 1, 0, cfg, BF16)
        tn_in = _pick_tile(math.gcd(dsb, dfx), 512, LANES)
        qs = 1.0 / math.sqrt(HEAD_DIM)
        col_scales = ((0, dsb // tn_in, qs), (3 * dsb // tn_in, (3 * dsb + dfx) // tn_in, qs * LOG2E))
        pf32, pbf = _mm(h, w_in, l, c0, (F32, BF16), tm_mm, tn_in, col_scales=col_scales)
        (tail,) = _mm(h, w_tail, l, ntail, (F32,), tm_mm, _pick_tile(ntail, 256, LANES))
        cqn, ckv, ckvb, kr, logf = _tailpost(tail, g_q3, g_kv3, b_f3, cos, sin, l, cfg)
        qcat = _qcat(cqn, w_uq_cat, cos, sin, l, cfg)
        kcat, vm = _kcat(ckvb, kr, w_ukf, w_uvf, l, cfg, tm_mm)

        cum_p = _cumsum(jnp.transpose(logf[:t])[None], LOG2E)[0]
        lf_s = jnp.concatenate([
            jnp.transpose(cache_fox_logf[l], (0, 2, 1)),
            jnp.transpose(logf[t:].reshape(bs, ts, hf), (0, 2, 1)),
            jnp.zeros((bs, hf, LANES - ts), F32)], axis=2)
        cum_s = _cumsum(lf_s, LOG2E)

        o = _sb_prompt(pbf, None, cfg, 0, nsb, 2 * nsb, 0)
        o = _sb_dec(pbf, cache_sb_k, cache_sb_v, l, o, cfg, 0, nsb, 2 * nsb, 0)
        f0 = 3 * nsb
        o = _fox_prompt(pbf, cum_p[:, None, :], o, cfg, f0, f0 + nfx, f0 + 2 * nfx, nsb)
        o = _fox_dec(pbf, cache_fox_k, cache_fox_v, cum_s[:, :, None, :], l, o, cfg,
                     f0, f0 + nfx, f0 + 2 * nfx, nsb)
        o = _mla_prompt(qcat, kcat, vm, o, cfg, nsb + nfx)
        o = _mla_dec(qcat, ckvb, kr, cache_mla_ckv, cache_mla_krope, w_ukf, w_uvf, l, o, cfg, nsb + nfx)

        x = _gnwo(o, g_out3, w_o, x, mods, l, cfg)
        x = _moe(x, g_mlp3, mods, w_router_p, b_router3, w_gate, b_gate4, w_up, b_up4, w_down, b_down4, l, cfg)

        new_rows = (pf32[:, dsb:2 * dsb], pf32[:, 2 * dsb:3 * dsb],
                    pf32[:, 3 * dsb + dfx:3 * dsb + 2 * dfx], pf32[:, 3 * dsb + 2 * dfx:],
                    logf, ckv, kr)
        for s, nr in zip(states, new_rows):
            s.append(nr)

    y_p, y_s = _normmod(x, g_final[None, None, :], modf, 0, 1, 0, cfg, F32, split_out=True)

    def stack(rows, shape_tail):
        return (jnp.stack([a[:t].reshape((1, t) + shape_tail) for a in rows]),
                jnp.stack([a[t:].reshape((bs, ts) + shape_tail) for a in rows]))

    tails = [(hs, HEAD_DIM), (hs, HEAD_DIM), (hf, HEAD_DIM), (hf, HEAD_DIM), (hf,), (kvr,), (rope,)]
    st_p, st_s = zip(*[stack(r, tl) for r, tl in zip(states, tails)])
    return (y_p.reshape(1, t, d), y_s.reshape(bs, ts, d)) + tuple(st_p) + tuple(st_s)
```
